```python
import math
import jax, jax.numpy as jnp
from jax import lax
import numpy as np

D_MODEL = 1024
BATCH = 4
SEQ = 8192
DEPTH = 4

N_MIXERS = 2
N_ATTN_LAYERS = (DEPTH + 1) // 2
N_SSD_LAYERS = DEPTH // 2
N_HEADS = 16
N_KV_HEADS = 4
HEAD_DIM = 64
GQA_GROUP = N_HEADS // N_KV_HEADS
WINDOW = 128
ATTN_BLOCK = 128
ROT_DIM = HEAD_DIM // 4
ROPE_THETA = 500000.0
Q_DIM = N_HEADS * HEAD_DIM
KV_DIM = N_KV_HEADS * HEAD_DIM
QKV_DIM = Q_DIM + 2 * KV_DIM
SSD_EXPAND = 2
D_INNER = SSD_EXPAND * D_MODEL
SSD_HEAD_DIM = 64
SSD_HEADS = D_INNER // SSD_HEAD_DIM
SSD_GROUPS = 8
HEADS_PER_GROUP = SSD_HEADS // SSD_GROUPS
D_STATE = 128
SSD_CONV = 5
SSD_CHUNK = 128
CONV_DIM = D_INNER + 2 * SSD_GROUPS * D_STATE
SSD_IN_DIM = D_INNER + CONV_DIM + 2 * SSD_HEADS
D_FF = 2816
FFN_CONV = 3
EPS = 1e-6

kernel_name = "bidir_swa_sink_ssd_convffn_hybrid"


def rmsnorm(x, g):
    xf = x.astype(jnp.float32)
    y = xf * lax.rsqrt(jnp.mean(xf * xf, axis=-1, keepdims=True) + EPS)
    return (y * g.astype(jnp.float32)).astype(x.dtype)


def dwconv_centred(x, w, b):
    k_w = w.shape[0]
    pad = k_w // 2
    s = x.shape[1]
    xp = jnp.pad(x, ((0, 0), (pad, pad), (0, 0)))
    y = b + xp[:, 0:s] * w[0]
    for k in range(1, k_w):
        y = y + xp[:, k:k + s] * w[k]
    return y


def rope_partial(t, cos, sin):
    half = ROT_DIM // 2
    c = cos[None, :, None, :]
    s = sin[None, :, None, :]
    t1 = t[..., :half].astype(jnp.float32)
    t2 = t[..., half:ROT_DIM].astype(jnp.float32)
    rot = jnp.concatenate([t1 * c - t2 * s, t2 * c + t1 * s], axis=-1).astype(t.dtype)
    return jnp.concatenate([rot, t[..., ROT_DIM:]], axis=-1)


def window_attention(x, norm_g, w_qkv, q_g, k_g, sink, w_o, cos, sin):
    bsz, s_len, _ = x.shape
    nb = s_len // ATTN_BLOCK
    h = rmsnorm(x, norm_g)
    qkv = h @ w_qkv
    q = qkv[..., :Q_DIM].reshape(bsz, s_len, N_HEADS, HEAD_DIM)
    k = qkv[..., Q_DIM:Q_DIM + KV_DIM].reshape(bsz, s_len, N_KV_HEADS, HEAD_DIM)
    v = qkv[..., Q_DIM + KV_DIM:].reshape(bsz, s_len, N_KV_HEADS, HEAD_DIM)
    q = rope_partial(rmsnorm(q, q_g), cos, sin)
    k = rope_partial(rmsnorm(k, k_g), cos, sin)
    q = q.reshape(bsz, nb, ATTN_BLOCK, N_KV_HEADS, GQA_GROUP, HEAD_DIM)

    def band(t):
        tp = jnp.pad(t, ((0, 0), (ATTN_BLOCK, ATTN_BLOCK), (0, 0), (0, 0)))
        tp = tp.reshape(bsz, nb + 2, ATTN_BLOCK, N_KV_HEADS, HEAD_DIM)
        return jnp.concatenate([tp[:, :-2], tp[:, 1:-1], tp[:, 2:]], axis=2)

    kb = band(k)
    vb = band(v)
    scale = HEAD_DIM ** -0.5
    sc = jnp.einsum('bnqkgd,bntkd->bnkgqt', q, kb).astype(jnp.float32) * scale
    blk = jnp.arange(nb)[:, None] * ATTN_BLOCK
    qpos = blk + jnp.arange(ATTN_BLOCK)[None, :]
    kpos = blk - ATTN_BLOCK + jnp.arange(3 * ATTN_BLOCK)[None, :]
    valid = (jnp.abs(qpos[:, :, None] - kpos[:, None, :]) <= WINDOW) \
        & (kpos >= 0)[:, None, :] & (kpos < s_len)[:, None, :]
    sc = jnp.where(valid[None, :, None, None], sc, -1e30)
    sink_l = sink.astype(jnp.float32).reshape(N_KV_HEADS, GQA_GROUP)[None, None, :, :, None, None]
    m = jnp.maximum(jnp.max(sc, axis=-1, keepdims=True), sink_l)
    p = jnp.exp(sc - m)
    denom = jnp.sum(p, axis=-1, keepdims=True) + jnp.exp(sink_l - m)
    p = (p / denom).astype(v.dtype)
    o = jnp.einsum('bnkgqt,bntkd->bnqkgd', p, vb).reshape(bsz, s_len, Q_DIM)
    return o @ w_o


def ssd_chunked(x, dt, a_diag, bm, cm):
    b, s_len, _, p = x.shape
    c = s_len // SSD_CHUNK
    xf = x.astype(jnp.float32)
    xdt = (xf * dt[..., None]).reshape(b, c, SSD_CHUNK, SSD_GROUPS, HEADS_PER_GROUP, p)
    a = (dt * a_diag).reshape(b, c, SSD_CHUNK, SSD_GROUPS, HEADS_PER_GROUP)
    bc = bm.astype(jnp.float32).reshape(b, c, SSD_CHUNK, SSD_GROUPS, D_STATE)
    cc = cm.astype(jnp.float32).reshape(b, c, SSD_CHUNK, SSD_GROUPS, D_STATE)
    a_cum = jnp.cumsum(a, axis=2)
    diff = a_cum[:, :, :, None] - a_cum[:, :, None, :]
    lower = jnp.tril(jnp.ones((SSD_CHUNK, SSD_CHUNK), dtype=bool))[:, :, None, None]
    decay_mat = jnp.exp(jnp.where(lower, diff, -jnp.inf))
    cb = jnp.einsum('bclgn,bcsgn->bclsg', cc, bc)
    y_diag = jnp.einsum('bclsgr,bcsgrp->bclgrp', cb[..., None] * decay_mat, xdt)
    decay_to_end = jnp.exp(a_cum[:, :, -1:] - a_cum)
    states = jnp.einsum('bclgn,bclgrp->bcgrpn', bc, xdt * decay_to_end[..., None])
    chunk_decay = jnp.exp(a_cum[:, :, -1])

    def step(h, inp):
        st, dc = inp
        return dc[..., None, None] * h + st, h

    h0 = jnp.zeros((b, SSD_GROUPS, HEADS_PER_GROUP, p, D_STATE), jnp.float32)
    _, prev = lax.scan(step, h0, (jnp.moveaxis(states, 1, 0), jnp.moveaxis(chunk_decay, 1, 0)))
    prev = jnp.moveaxis(prev, 0, 1)
    y_off = jnp.einsum('bclgn,bcgrpn->bclgrp', cc, prev) * jnp.exp(a_cum)[..., None]
    return (y_diag + y_off).reshape(b, s_len, SSD_HEADS, p)


def ssd_mixer(x, norm_g, w_in, conv_w, conv_b, dt_bias, a_log, d_skip, gate_g, w_out):
    bsz, s_len, _ = x.shape
    h = rmsnorm(x, norm_g)
    zxbcdt = h @ w_in
    z = zxbcdt[..., :D_INNER]
    xbc = zxbcdt[..., D_INNER:D_INNER + CONV_DIM]
    dt_raw = zxbcdt[..., D_INNER + CONV_DIM:]
    xbc = jax.nn.silu(dwconv_centred(xbc, conv_w, conv_b))
    gn = SSD_GROUPS * D_STATE
    xs = xbc[..., :D_INNER].reshape(bsz, s_len, SSD_HEADS, SSD_HEAD_DIM)
    bm = xbc[..., D_INNER:D_INNER + gn].reshape(bsz, s_len, SSD_GROUPS, D_STATE)
    cm = xbc[..., D_INNER + gn:].reshape(bsz, s_len, SSD_GROUPS, D_STATE)
    dt = jax.nn.softplus(dt_raw.astype(jnp.float32).reshape(bsz, s_len, 2, SSD_HEADS)
                         + dt_bias.astype(jnp.float32))
    a_diag = -jnp.exp(a_log.astype(jnp.float32))
    flip = lambda t: jnp.flip(t, axis=1)
    y_fwd = ssd_chunked(xs, dt[:, :, 0], a_diag[0], bm, cm)
    y_bwd = flip(ssd_chunked(flip(xs), flip(dt[:, :, 1]), a_diag[1], flip(bm), flip(cm)))
    y = y_fwd + y_bwd + xs.astype(jnp.float32) * d_skip.astype(jnp.float32)[:, None]
    y = y.reshape(bsz, s_len, D_INNER) * jax.nn.silu(z.astype(jnp.float32))
    yg = y.reshape(bsz, s_len, SSD_GROUPS, D_INNER // SSD_GROUPS)
    yg = yg * lax.rsqrt(jnp.mean(yg * yg, axis=-1, keepdims=True) + EPS)
    y = yg.reshape(bsz, s_len, D_INNER) * gate_g.astype(jnp.float32)
    return y.astype(x.dtype) @ w_out


def conv_ffn(x, norm_g, w_up, conv_w, conv_b, w_down):
    h = rmsnorm(x, norm_g) @ w_up
    h = dwconv_centred(h, conv_w, conv_b)
    gate = h[..., :D_FF]
    val = h[..., D_FF:]
    return (jax.nn.silu(gate) * val) @ w_down


def setup_inputs(seed: int = 0) -> dict:
    key = jax.random.key(seed)
    ks = jax.random.split(key, 24)
    f32 = jnp.float32

    def nrm(k, shape, scale):
        return jax.random.normal(k, shape, f32) * scale

    na, ns = N_ATTN_LAYERS, N_SSD_LAYERS
    dt0 = jnp.exp(jax.random.uniform(ks[11], (ns, 2, SSD_HEADS), f32,
                                     minval=math.log(1e-3), maxval=math.log(1e-1)))
    return {
        "x": nrm(ks[0], (BATCH, SEQ, D_MODEL), 1.0),
        "attn_norm": 1.0 + nrm(ks[1], (na, D_MODEL), 0.02),
        "attn_w_qkv": nrm(ks[2], (na, D_MODEL, QKV_DIM), D_MODEL ** -0.5),
        "attn_q_norm": 1.0 + nrm(ks[3], (na, HEAD_DIM), 0.02),
        "attn_k_norm": 1.0 + nrm(ks[4], (na, HEAD_DIM), 0.02),
        "attn_sink": nrm(ks[5], (na, N_HEADS), 0.5),
        "attn_w_o": nrm(ks[6], (na, Q_DIM, D_MODEL), Q_DIM ** -0.5),
        "ssd_norm": 1.0 + nrm(ks[7], (ns, D_MODEL), 0.02),
        "ssd_w_in": nrm(ks[8], (ns, D_MODEL, SSD_IN_DIM), D_MODEL ** -0.5),
        "ssd_conv_w": nrm(ks[9], (ns, SSD_CONV, CONV_DIM), SSD_CONV ** -0.5),
        "ssd_conv_b": nrm(ks[10], (ns, CONV_DIM), 0.02),
        "ssd_dt_bias": dt0 + jnp.log(-jnp.expm1(-dt0)),
        "ssd_a_log": jnp.log(jax.random.uniform(ks[12], (ns, 2, SSD_HEADS), f32, minval=1.0, maxval=16.0)),
        "ssd_d": 1.0 + nrm(ks[13], (ns, SSD_HEADS), 0.1),
        "ssd_gate_norm": 1.0 + nrm(ks[14], (ns, D_INNER), 0.02),
        "ssd_w_out": nrm(ks[15], (ns, D_INNER, D_MODEL), D_INNER ** -0.5),
        "ffn_norm": 1.0 + nrm(ks[16], (DEPTH, D_MODEL), 0.02),
        "ffn_w_up": nrm(ks[17], (DEPTH, D_MODEL, 2 * D_FF), D_MODEL ** -0.5),
        "ffn_conv_w": nrm(ks[18], (DEPTH, FFN_CONV, 2 * D_FF), FFN_CONV ** -0.5),
        "ffn_conv_b": nrm(ks[19], (DEPTH, 2 * D_FF), 0.02),
        "ffn_w_down": nrm(ks[20], (DEPTH, D_FF, D_MODEL), D_FF ** -0.5),
    }


def reference(x, attn_norm, attn_w_qkv, attn_q_norm, attn_k_norm, attn_sink, attn_w_o,
              ssd_norm, ssd_w_in, ssd_conv_w, ssd_conv_b, ssd_dt_bias, ssd_a_log, ssd_d,
              ssd_gate_norm, ssd_w_out,
              ffn_norm, ffn_w_up, ffn_conv_w, ffn_conv_b, ffn_w_down):
    s_len = x.shape[1]
    pos = jnp.arange(s_len, dtype=jnp.float32)
    inv_freq = ROPE_THETA ** (-(jnp.arange(0, ROT_DIM, 2, dtype=jnp.float32) / ROT_DIM))
    ang = pos[:, None] * inv_freq[None, :]
    cos, sin = jnp.cos(ang), jnp.sin(ang)
    for i in range(DEPTH):
        j = i // N_MIXERS
        if i % N_MIXERS == 0:
            x = x + window_attention(x, attn_norm[j], attn_w_qkv[j], attn_q_norm[j],
                                     attn_k_norm[j], attn_sink[j], attn_w_o[j], cos, sin)
        else:
            x = x + ssd_mixer(x, ssd_norm[j], ssd_w_in[j], ssd_conv_w[j], ssd_conv_b[j],
                              ssd_dt_bias[j], ssd_a_log[j], ssd_d[j], ssd_gate_norm[j], ssd_w_out[j])
        x = x + conv_ffn(x, ffn_norm[i], ffn_w_up[i], ffn_conv_w[i], ffn_conv_b[i], ffn_w_down[i])
    return x
```

```python
import functools
import math

import jax
import jax.numpy as jnp
from jax import lax
from jax.experimental import pallas as pl
from jax.experimental.pallas import tpu as pltpu

F32 = jnp.float32
BF16 = jnp.bfloat16

EPS = 1e-6
HEAD_DIM = 64
ROT_DIM = HEAD_DIM // 4
ROPE_THETA = 500000.0
ATTN_BLOCK = 128
SSD_HEAD_DIM = 64
SSD_GROUPS = 8
D_STATE = 128
SSD_CHUNK = 128

V7X_LANES = 128
BF16_SUBLANES = 16
V7X_VMEM_LIMIT = 56 * 1024 * 1024


def _rms(x, g):
    return x * lax.rsqrt(jnp.mean(x * x, axis=-1, keepdims=True) + EPS) * g


def _silu(x):
    return x * (1.0 / (1.0 + jnp.exp(-x)))


def _dot(a, b):
    return jnp.dot(a, b, preferred_element_type=F32)


def _dot_nt(a, b):
    return lax.dot_general(a, b, (((1,), (1,)), ((), ())), preferred_element_type=F32)


def _dot_tn(a, b):
    return lax.dot_general(a, b, (((0,), (0,)), ((), ())), preferred_element_type=F32)


def _split3(x):
    p1 = x.astype(BF16)
    r1 = x - p1.astype(F32)
    p2 = r1.astype(BF16)
    p3 = (r1 - p2.astype(F32)).astype(BF16)
    return p1, p2, p3


def _halo_specs(tm, s_len, d):
    r = tm // BF16_SUBLANES
    last = s_len // BF16_SUBLANES - 1
    return [
        pl.BlockSpec((1, BF16_SUBLANES, d), lambda b, i: (b, jnp.maximum(i * r - 1, 0), 0)),
        pl.BlockSpec((1, tm, d), lambda b, i: (b, i, 0)),
        pl.BlockSpec((1, BF16_SUBLANES, d), lambda b, i: (b, jnp.minimum((i + 1) * r, last), 0)),
    ]


def _full(shape):
    return pl.BlockSpec(shape, lambda b, i: (0,) * len(shape))


def _fill_normed(hn_scr, xp_ref, x_ref, xn_ref, g, tm):
    i = pl.program_id(1)
    n = pl.num_programs(1)
    h = BF16_SUBLANES
    keep_p = (i > 0).astype(F32)
    keep_n = (i < n - 1).astype(F32)
    hn_scr[0:h, :] = (_rms(xp_ref[0], g) * keep_p).astype(BF16)
    hn_scr[h:h + tm, :] = _rms(x_ref[0], g).astype(BF16)
    hn_scr[h + tm:h + tm + h, :] = (_rms(xn_ref[0], g) * keep_n).astype(BF16)


def _dwconv(h, w, b, tm):
    k_w = w.shape[0]
    off = BF16_SUBLANES - k_w // 2
    y = b + h[off:off + tm] * w[0:1]
    for k in range(1, k_w):
        y = y + h[off + k:off + k + tm] * w[k:k + 1]
    return y


def _ffn_kernel(xp_ref, x_ref, xn_ref, g_ref, wup_ref, cw_ref, cb_ref, wdn_ref, o_ref, hn_scr,
                *, tm, fc, d_ff):
    _fill_normed(hn_scr, xp_ref, x_ref, xn_ref, g_ref[...], tm)
    hn = hn_scr[...]
    acc = x_ref[0]
    for c in range(d_ff // fc):
        lo, hi = c * fc, (c + 1) * fc
        hg = _dot(hn, wup_ref[:, lo:hi])
        hv = _dot(hn, wup_ref[:, d_ff + lo:d_ff + hi])
        gate = _dwconv(hg, cw_ref[:, lo:hi], cb_ref[:, lo:hi], tm)
        val = _dwconv(hv, cw_ref[:, d_ff + lo:d_ff + hi], cb_ref[:, d_ff + lo:d_ff + hi], tm)
        act = (_silu(gate) * val).astype(BF16)
        acc = acc + _dot(act, wdn_ref[lo:hi, :])
    o_ref[0] = acc


def _conv_ffn(x, g, w_up, conv_w, conv_b, w_down, *, tm=512, fc=256):
    bsz, s_len, d = x.shape
    d_ff = w_down.shape[0]
    tm = min(tm, s_len)
    kern = functools.partial(_ffn_kernel, tm=tm, fc=fc, d_ff=d_ff)
    return pl.pallas_call(
        kern,
        grid=(bsz, s_len // tm),
        in_specs=_halo_specs(tm, s_len, d) + [
            _full((1, d)), _full((d, 2 * d_ff)), _full(conv_w.shape), _full((1, 2 * d_ff)),
            _full((d_ff, d))],
        out_specs=pl.BlockSpec((1, tm, d), lambda b, i: (b, i, 0)),
        out_shape=jax.ShapeDtypeStruct(x.shape, F32),
        scratch_shapes=[pltpu.VMEM((tm + 2 * BF16_SUBLANES, d), BF16)],
        compiler_params=pltpu.CompilerParams(
            dimension_semantics=("parallel", "parallel"), vmem_limit_bytes=V7X_VMEM_LIMIT),
        name="conv_ffn",
    )(x, x, x, g.reshape(1, d), w_up.astype(BF16), conv_w, conv_b.reshape(1, -1),
      w_down.astype(BF16))


def _rope_tables(s_len):
    half = ROT_DIM // 2
    pos = jnp.arange(s_len, dtype=F32)
    inv_freq = ROPE_THETA ** (-(jnp.arange(0, ROT_DIM, 2, dtype=F32) / ROT_DIM))
    ang = pos[:, None] * inv_freq[None, :]
    cos, sin = jnp.cos(ang), jnp.sin(ang)
    rest = HEAD_DIM - ROT_DIM
    c = jnp.concatenate([cos, cos, jnp.ones((s_len, rest), F32)], axis=1)
    s1 = jnp.concatenate([-sin, jnp.zeros((s_len, half + rest), F32)], axis=1)
    s2 = jnp.concatenate([jnp.zeros((s_len, half), F32), sin, jnp.zeros((s_len, rest), F32)], axis=1)
    rep = V7X_LANES // HEAD_DIM
    return jnp.tile(c, (1, rep)), jnp.tile(s1, (1, rep)), jnp.tile(s2, (1, rep))


def _norm_rope(t, bd, g, c, s1, s2, scale):
    half = ROT_DIM // 2
    msq = _dot((t * t).astype(BF16), bd)
    tn = t * lax.rsqrt(msq + EPS) * g
    outs = []
    for j in range(t.shape[1] // V7X_LANES):
        tc = tn[:, j * V7X_LANES:(j + 1) * V7X_LANES]
        tr = tc * c + pltpu.roll(tc, V7X_LANES - half, 1) * s1 + pltpu.roll(tc, half, 1) * s2
        outs.append(tr * scale if scale != 1.0 else tr)
    return jnp.concatenate(outs, axis=1)


def _qkv_kernel(x_ref, g_ref, w_ref, qg_ref, kg_ref, bdq_ref, bdk_ref, rep_ref, c_ref, s1_ref,
                s2_ref, q_out, k_out, v_out, *, q_dim, kv_dim):
    hn = _rms(x_ref[0], g_ref[...]).astype(BF16)
    c, s1, s2 = c_ref[...], s1_ref[...], s2_ref[...]
    q = _dot(hn, w_ref[:, 0:q_dim])
    q_out[0] = _norm_rope(q, bdq_ref[...], qg_ref[...], c, s1, s2, HEAD_DIM ** -0.5).astype(BF16)
    k = _dot(hn, w_ref[:, q_dim:q_dim + kv_dim])
    kr = _norm_rope(k, bdk_ref[...], kg_ref[...], c, s1, s2, 1.0).astype(BF16)
    k_out[0] = _dot(kr, rep_ref[...]).astype(BF16)
    v = _dot(hn, w_ref[:, q_dim + kv_dim:q_dim + 2 * kv_dim]).astype(BF16)
    v_out[0] = _dot(v, rep_ref[...]).astype(BF16)


def _attn_kernel(sink_ref, q_ref, kp_ref, k_ref, kn_ref, vp_ref, v_ref, vn_ref, x_ref, wo_ref,
                 o_ref, o_scr, *, tq, n_kv, group):
    i = pl.program_id(1)
    n = pl.num_programs(1)
    blk = ATTN_BLOCK
    gw = group * HEAD_DIM
    k_all = jnp.concatenate([kp_ref[0], k_ref[0], kn_ref[0]], axis=0)
    v_all = jnp.concatenate([vp_ref[0], v_ref[0], vn_ref[0]], axis=0)
    row = lax.broadcasted_iota(jnp.int32, (blk, 3 * blk), 0)
    col = lax.broadcasted_iota(jnp.int32, (blk, 3 * blk), 1)
    band = (col >= row) & (col <= row + 2 * blk)
    head_of_lane = lax.broadcasted_iota(jnp.int32, (blk, gw), 1) // HEAD_DIM
    nblk = tq // blk
    for jb in range(nblk):
        valid = band
        if jb == 0:
            valid = valid & ((col >= blk) | (i > 0))
        if jb == nblk - 1:
            valid = valid & ((col < 2 * blk) | (i < n - 1))
        for kh in range(n_kv):
            qg = q_ref[0, jb * blk:(jb + 1) * blk, kh * gw:(kh + 1) * gw]
            kw = k_all[jb * blk:(jb + 3) * blk, kh * gw:(kh + 1) * gw]
            vw = v_all[jb * blk:(jb + 3) * blk, kh * gw:(kh + 1) * gw]
            og = jnp.zeros((blk, gw), F32)
            for hl in range(group):
                mine = head_of_lane == hl
                s = _dot_nt(jnp.where(mine, qg, jnp.zeros_like(qg)), kw)
                s = jnp.where(valid, s, -1e30)
                sink = sink_ref[kh * group + hl]
                m = jnp.maximum(jnp.max(s, axis=-1, keepdims=True), sink)
                p = jnp.exp(s - m)
                denom = jnp.sum(p, axis=-1, keepdims=True) + jnp.exp(sink - m)
                o = _dot(p.astype(BF16), vw)
                og = jnp.where(mine, o / denom, og)
            o_scr[jb * blk:(jb + 1) * blk, kh * gw:(kh + 1) * gw] = og.astype(BF16)
    o_ref[0] = x_ref[0] + _dot(o_scr[...], wo_ref[...])


def _window_attention(x, norm_g, w_qkv, q_g, k_g, sink, w_o, rope, *, tm=512, tq=512):
    bsz, s_len, d = x.shape
    n_heads = sink.shape[0]
    q_dim = w_o.shape[0]
    kv_dim = (w_qkv.shape[1] - q_dim) // 2
    n_kv = kv_dim // HEAD_DIM
    group = n_heads // n_kv
    tm = min(tm, s_len)
    tq = min(tq, s_len)
    c, s1, s2 = rope

    def block_diag(width):
        idx = jnp.arange(width) // HEAD_DIM
        return jnp.where(idx[:, None] == idx[None, :], 1.0 / HEAD_DIM, 0.0).astype(BF16)

    src = jnp.arange(kv_dim)
    dst = jnp.arange(q_dim)
    rep = ((src[:, None] // HEAD_DIM == dst[None, :] // (group * HEAD_DIM))
           & (src[:, None] % HEAD_DIM == dst[None, :] % HEAD_DIM)).astype(BF16)

    tile = lambda w: pl.BlockSpec((1, tm, w), lambda b, i: (b, i, 0))
    rope_spec = pl.BlockSpec((tm, V7X_LANES), lambda b, i: (i, 0))
    q, k, v = pl.pallas_call(
        functools.partial(_qkv_kernel, q_dim=q_dim, kv_dim=kv_dim),
        grid=(bsz, s_len // tm),
        in_specs=[tile(d), _full((1, d)), _full(w_qkv.shape), _full((1, q_dim)),
                  _full((1, kv_dim)), _full((q_dim, q_dim)), _full((kv_dim, kv_dim)),
                  _full((kv_dim, q_dim)), rope_spec, rope_spec, rope_spec],
        out_specs=[tile(q_dim), tile(q_dim), tile(q_dim)],
        out_shape=[jax.ShapeDtypeStruct((bsz, s_len, q_dim), BF16)] * 3,
        compiler_params=pltpu.CompilerParams(
            dimension_semantics=("parallel", "parallel"), vmem_limit_bytes=V7X_VMEM_LIMIT),
        name="attn_qkv",
    )(x, norm_g.reshape(1, d), w_qkv.astype(BF16), jnp.tile(q_g, n_heads).reshape(1, q_dim),
      jnp.tile(k_g, n_kv).reshape(1, kv_dim), block_diag(q_dim), block_diag(kv_dim), rep, c, s1, s2)

    r = tq // ATTN_BLOCK
    last = s_len // ATTN_BLOCK - 1
    prev_spec = pl.BlockSpec((1, ATTN_BLOCK, q_dim), lambda b, i: (b, jnp.maximum(i * r - 1, 0), 0))
    main_spec = pl.BlockSpec((1, tq, q_dim), lambda b, i: (b, i, 0))
    next_spec = pl.BlockSpec((1, ATTN_BLOCK, q_dim), lambda b, i: (b, jnp.minimum((i + 1) * r, last), 0))
    return pl.pallas_call(
        functools.partial(_attn_kernel, tq=tq, n_kv=n_kv, group=group),
        grid=(bsz, s_len // tq),
        in_specs=[pl.BlockSpec(memory_space=pltpu.SMEM), main_spec,
                  prev_spec, main_spec, next_spec, prev_spec, main_spec, next_spec,
                  pl.BlockSpec((1, tq, d), lambda b, i: (b, i, 0)), _full(w_o.shape)],
        out_specs=pl.BlockSpec((1, tq, d), lambda b, i: (b, i, 0)),
        out_shape=jax.ShapeDtypeStruct(x.shape, F32),
        scratch_shapes=[pltpu.VMEM((tq, q_dim), BF16)],
        compiler_params=pltpu.CompilerParams(
            dimension_semantics=("parallel", "parallel"), vmem_limit_bytes=V7X_VMEM_LIMIT),
        name="attn_core",
    )(sink.astype(F32), q, k, k, k, v, v, v, x, w_o.astype(BF16))


def _ssd_in_kernel(xp_ref, x_ref, xn_ref, g_ref, w_ref, wdt_ref, cw_ref, cb_ref, dtb_ref, dtbt_ref,
                   z_out, xbc_out, dt_out, dtt_out, hn_scr, *, tm, fc, d_inner, conv_dim):
    _fill_normed(hn_scr, xp_ref, x_ref, xn_ref, g_ref[...], tm)
    h0 = BF16_SUBLANES
    hn = hn_scr[...]
    hn_main = hn_scr[h0:h0 + tm, :]
    z_out[0] = _silu(_dot(hn_main, w_ref[:, 0:d_inner])).astype(BF16)
    for c in range(conv_dim // fc):
        lo, hi = c * fc, (c + 1) * fc
        h = _dot(hn, w_ref[:, d_inner + lo:d_inner + hi])
        xbc_out[0, :, lo:hi] = _silu(_dwconv(h, cw_ref[:, lo:hi], cb_ref[:, lo:hi], tm)).astype(BF16)
    dt_raw = _dot(hn_main, w_ref[:, d_inner + conv_dim:]) + dtb_ref[...]
    dt_out[0] = jnp.maximum(dt_raw, 0.0) + jnp.log1p(jnp.exp(-jnp.abs(dt_raw)))
    dtt_raw = _dot_nt(wdt_ref[...], hn_main) + dtbt_ref[...]
    dtt_out[0] = jnp.maximum(dtt_raw, 0.0) + jnp.log1p(jnp.exp(-jnp.abs(dtt_raw)))


def _expand_heads(v, g, lane_head, hpg):
    out = v[:, g * hpg + hpg - 1:g * hpg + hpg]
    for r in range(hpg - 2, -1, -1):
        out = jnp.where(lane_head == r, v[:, g * hpg + r:g * hpg + r + 1], out)
    return out


def _ssd_chunk(xs, bm, cm, dt, dtt, a_diag, a_diag_t, state_ref, reverse):
    n_chunk, d_inner = xs.shape
    heads = dt.shape[1]
    hpg = heads // SSD_GROUPS
    gw = hpg * SSD_HEAD_DIM
    row = lax.broadcasted_iota(jnp.int32, (n_chunk, n_chunk), 0)
    col = lax.broadcasted_iota(jnp.int32, (n_chunk, n_chunk), 1)
    later = (row <= col) if reverse else (row >= col)
    tri = later.astype(BF16)
    tri_t = ((row >= col) if reverse else (row <= col)).astype(BF16)
    a = dt * a_diag
    a_t = dtt * a_diag_t
    p1, p2, p3 = _split3(a)
    a_cum = (_dot(tri, p1) + _dot(tri, p2)) + _dot(tri, p3)
    q1, q2, q3 = _split3(a_t)
    a_cum_t = (_dot(q1, tri_t) + _dot(q2, tri_t)) + _dot(q3, tri_t)
    end = 0 if reverse else n_chunk - 1
    a_end = a_cum[end:end + 1, :]
    e_cum = jnp.exp(a_cum)
    w_in = dt * jnp.exp(a_end - a_cum)
    lane_head = lax.broadcasted_iota(jnp.int32, (n_chunk, gw), 1) // SSD_HEAD_DIM
    ys = []
    for g in range(SSD_GROUPS):
        b_g = bm[:, g * D_STATE:(g + 1) * D_STATE]
        c_g = cm[:, g * D_STATE:(g + 1) * D_STATE]
        x_g = xs[:, g * gw:(g + 1) * gw]
        cb = _dot_nt(c_g, b_g)
        ms, xb = [], []
        for r in range(hpg):
            h = g * hpg + r
            diff = a_cum[:, h:h + 1] - a_cum_t[h:h + 1, :]
            decay = jnp.exp(jnp.where(later, diff, -1e30))
            ms.append((cb * decay * dtt[h:h + 1, :]).astype(BF16))
            xb.append(jnp.where(lane_head == r, x_g, jnp.zeros_like(x_g)))
        y_diag = _dot(jnp.concatenate(ms, axis=1), jnp.concatenate(xb, axis=0))
        st = state_ref[g]
        y_off = _dot(c_g, st.astype(BF16)) * _expand_heads(e_cum, g, lane_head, hpg)
        ys.append(y_diag + y_off)
        xw = (x_g.astype(F32) * _expand_heads(w_in, g, lane_head, hpg)).astype(BF16)
        chunk_decay = _expand_heads(jnp.exp(a_end), g, lane_head[0:1], hpg)
        state_ref[g] = chunk_decay * st + _dot_tn(b_g, xw)
    return jnp.concatenate(ys, axis=1)


def _ssd_bwd_kernel(xbc_ref, dt_ref, dtt_ref, a_ref, at_ref, y_out, state_ref, *, d_inner, heads):
    @pl.when(pl.program_id(1) == 0)
    def _():
        state_ref[...] = jnp.zeros_like(state_ref)

    gn = SSD_GROUPS * D_STATE
    xs = xbc_ref[0, :, 0:d_inner]
    bm = xbc_ref[0, :, d_inner:d_inner + gn]
    cm = xbc_ref[0, :, d_inner + gn:d_inner + 2 * gn]
    y_out[0] = _ssd_chunk(xs, bm, cm, dt_ref[0, :, heads:2 * heads], dtt_ref[0, heads:2 * heads, :],
                          a_ref[1:2, :], at_ref[:, 1:2], state_ref, True)


def _ssd_fwd_kernel(xbc_ref, dt_ref, dtt_ref, a_ref, at_ref, yb_ref, z_ref, x_ref, dexp_ref, gg_ref,
                    wout_ref, o_ref, state_ref, yn_scr, *, d_inner, heads):
    @pl.when(pl.program_id(1) == 0)
    def _():
        state_ref[...] = jnp.zeros_like(state_ref)

    gn = SSD_GROUPS * D_STATE
    xs = xbc_ref[0, :, 0:d_inner]
    bm = xbc_ref[0, :, d_inner:d_inner + gn]
    cm = xbc_ref[0, :, d_inner + gn:d_inner + 2 * gn]
    y = _ssd_chunk(xs, bm, cm, dt_ref[0, :, 0:heads], dtt_ref[0, 0:heads, :],
                   a_ref[0:1, :], at_ref[:, 0:1], state_ref, False)
    y = y + yb_ref[0] + xs.astype(F32) * dexp_ref[...]
    y = y * z_ref[0].astype(F32)
    gw = d_inner // SSD_GROUPS
    for g in range(SSD_GROUPS):
        yg = y[:, g * gw:(g + 1) * gw]
        yg = yg * lax.rsqrt(jnp.mean(yg * yg, axis=-1, keepdims=True) + EPS)
        yn_scr[:, g * gw:(g + 1) * gw] = (yg * gg_ref[:, g * gw:(g + 1) * gw]).astype(BF16)
    o_ref[0] = x_ref[0] + _dot(yn_scr[...], wout_ref[...])


def _ssd_mixer(x, norm_g, w_in, conv_w, conv_b, dt_bias, a_log, d_skip, gate_g, w_out,
               *, tm=512, fc=512):
    bsz, s_len, d = x.shape
    d_inner = w_out.shape[0]
    heads = d_skip.shape[0]
    conv_dim = conv_w.shape[1]
    tm = min(tm, s_len)
    w_bf = w_in.astype(BF16)
    w_dt_t = w_bf[:, d_inner + conv_dim:].T
    tile = lambda w: pl.BlockSpec((1, tm, w), lambda b, i: (b, i, 0))
    sz, xbc, dt, dtt = pl.pallas_call(
        functools.partial(_ssd_in_kernel, tm=tm, fc=fc, d_inner=d_inner, conv_dim=conv_dim),
        grid=(bsz, s_len // tm),
        in_specs=_halo_specs(tm, s_len, d) + [
            _full((1, d)), _full(w_in.shape), _full((2 * heads, d)), _full(conv_w.shape),
            _full((1, conv_dim)), _full((1, 2 * heads)), _full((2 * heads, 1))],
        out_specs=[tile(d_inner), tile(conv_dim), tile(2 * heads),
                   pl.BlockSpec((1, 2 * heads, tm), lambda b, i: (b, 0, i))],
        out_shape=[jax.ShapeDtypeStruct((bsz, s_len, d_inner), BF16),
                   jax.ShapeDtypeStruct((bsz, s_len, conv_dim), BF16),
                   jax.ShapeDtypeStruct((bsz, s_len, 2 * heads), F32),
                   jax.ShapeDtypeStruct((bsz, 2 * heads, s_len), F32)],
        scratch_shapes=[pltpu.VMEM((tm + 2 * BF16_SUBLANES, d), BF16)],
        compiler_params=pltpu.CompilerParams(
            dimension_semantics=("parallel", "parallel"), vmem_limit_bytes=V7X_VMEM_LIMIT),
        name="ssd_in",
    )(x, x, x, norm_g.reshape(1, d), w_bf, w_dt_t, conv_w, conv_b.reshape(1, conv_dim),
      dt_bias.reshape(1, 2 * heads).astype(F32), dt_bias.reshape(2 * heads, 1).astype(F32))

    a_diag = -jnp.exp(a_log.astype(F32))
    a_diag_t = a_diag.T
    lc = SSD_CHUNK
    nc = s_len // lc
    hpg = heads // SSD_GROUPS
    state = pltpu.VMEM((SSD_GROUPS, D_STATE, hpg * SSD_HEAD_DIM), F32)
    rev = lambda w: pl.BlockSpec((1, lc, w), lambda b, c: (b, nc - 1 - c, 0))
    fwd = lambda w: pl.BlockSpec((1, lc, w), lambda b, c: (b, c, 0))
    y_b = pl.pallas_call(
        functools.partial(_ssd_bwd_kernel, d_inner=d_inner, heads=heads),
        grid=(bsz, nc),
        in_specs=[rev(conv_dim), rev(2 * heads),
                  pl.BlockSpec((1, 2 * heads, lc), lambda b, c: (b, 0, nc - 1 - c)),
                  _full((2, heads)), _full((heads, 2))],
        out_specs=rev(d_inner),
        out_shape=jax.ShapeDtypeStruct((bsz, s_len, d_inner), F32),
        scratch_shapes=[state],
        compiler_params=pltpu.CompilerParams(
            dimension_semantics=("parallel", "arbitrary"), vmem_limit_bytes=V7X_VMEM_LIMIT),
        name="ssd_scan_bwd",
    )(xbc, dt, dtt, a_diag, a_diag_t)

    return pl.pallas_call(
        functools.partial(_ssd_fwd_kernel, d_inner=d_inner, heads=heads),
        grid=(bsz, nc),
        in_specs=[fwd(conv_dim), fwd(2 * heads),
                  pl.BlockSpec((1, 2 * heads, lc), lambda b, c: (b, 0, c)),
                  _full((2, heads)), _full((heads, 2)), fwd(d_inner), fwd(d_inner), fwd(d),
                  _full((1, d_inner)), _full((1, d_inner)), _full(w_out.shape)],
        out_specs=fwd(d),
        out_shape=jax.ShapeDtypeStruct(x.shape, F32),
        scratch_shapes=[state, pltpu.VMEM((lc, d_inner), BF16)],
        compiler_params=pltpu.CompilerParams(
            dimension_semantics=("parallel", "arbitrary"), vmem_limit_bytes=V7X_VMEM_LIMIT),
        name="ssd_scan_fwd",
    )(xbc, dt, dtt, a_diag, a_diag_t, y_b, sz, x,
      jnp.repeat(d_skip.astype(F32), SSD_HEAD_DIM).reshape(1, d_inner),
      gate_g.reshape(1, d_inner).astype(F32), w_out.astype(BF16))


def kernel(x, attn_norm, attn_w_qkv, attn_q_norm, attn_k_norm, attn_sink, attn_w_o, ssd_norm, ssd_w_in, ssd_conv_w, ssd_conv_b, ssd_dt_bias, ssd_a_log, ssd_d, ssd_gate_norm, ssd_w_out, ffn_norm, ffn_w_up, ffn_conv_w, ffn_conv_b, ffn_w_down):
    depth = ffn_norm.shape[0]
    rope = _rope_tables(x.shape[1])
    for i in range(depth):
        j = i // 2
        if i % 2 == 0:
            x = _window_attention(x, attn_norm[j], attn_w_qkv[j], attn_q_norm[j], attn_k_norm[j],
                                  attn_sink[j], attn_w_o[j], rope)
        else:
            x = _ssd_mixer(x, ssd_norm[j], ssd_w_in[j], ssd_conv_w[j], ssd_conv_b[j],
                           ssd_dt_bias[j], ssd_a_log[j], ssd_d[j], ssd_gate_norm[j], ssd_w_out[j])
        x = _conv_ffn(x, ffn_norm[i], ffn_w_up[i], ffn_conv_w[i], ffn_conv_b[i], ffn_w_down[i])
    return x
```

```python
import functools

import jax
import jax.numpy as jnp
from jax import lax
from jax.experimental import pallas as pl
from jax.experimental.pallas import tpu as pltpu

F32 = jnp.float32
BF16 = jnp.bfloat16

EPS = 1e-6
HEAD_DIM = 64
ROT_DIM = HEAD_DIM // 4
ROPE_THETA = 500000.0
ATTN_BLOCK = 128
SSD_HEAD_DIM = 64
SSD_GROUPS = 8
D_STATE = 128
SSD_CHUNK = 128

V7X_LANES = 128
F32_SUBLANES = 8
V7X_VMEM_LIMIT = 56 * 1024 * 1024


def _rms(x, g):
    return x * lax.rsqrt(jnp.mean(x * x, axis=-1, keepdims=True) + EPS) * g


def _silu(x):
    return x * (1.0 / (1.0 + jnp.exp(-x)))


def _dot(a, b):
    return jnp.dot(a, b, preferred_element_type=F32)


def _dot_nt(a, b):
    return lax.dot_general(a, b, (((1,), (1,)), ((), ())), preferred_element_type=F32)


def _dot_tn(a, b):
    return lax.dot_general(a, b, (((0,), (0,)), ((), ())), preferred_element_type=F32)


def _halo_specs(tm, s_len, d):
    r = tm // F32_SUBLANES
    last = s_len // F32_SUBLANES - 1
    return [
        pl.BlockSpec((1, F32_SUBLANES, d), lambda b, i: (b, jnp.maximum(i * r - 1, 0), 0)),
        pl.BlockSpec((1, tm, d), lambda b, i: (b, i, 0)),
        pl.BlockSpec((1, F32_SUBLANES, d), lambda b, i: (b, jnp.minimum((i + 1) * r, last), 0)),
    ]


def _full(shape):
    return pl.BlockSpec(shape, lambda b, i: (0,) * len(shape))


def _perm_base(a, tm):
    p = tm // F32_SUBLANES
    t = a * F32_SUBLANES
    return (t % p) * F32_SUBLANES + t // p


def _fill_normed(hn_scr, perm_scr, xp_ref, x_ref, xn_ref, g, tm):
    i = pl.program_id(1)
    n = pl.num_programs(1)
    h = F32_SUBLANES
    keep_p = (i > 0).astype(F32)
    keep_n = (i < n - 1).astype(F32)
    halo = jnp.concatenate([_rms(xp_ref[0], g) * keep_p, _rms(xn_ref[0], g) * keep_n], axis=0)
    hn_scr[0:2 * h, :] = halo.astype(BF16)
    xn = _rms(x_ref[0], g)
    slabs = xn.shape[1] // V7X_LANES
    for a in range(tm // h):
        for j in range(slabs):
            perm_scr[j, pl.ds(_perm_base(a, tm), h, stride=h), :] = (
                xn[a * h:(a + 1) * h, j * V7X_LANES:(j + 1) * V7X_LANES])
    for j in range(slabs):
        hn_scr[2 * h:2 * h + tm, j * V7X_LANES:(j + 1) * V7X_LANES] = perm_scr[j].astype(BF16)


def _unpermute(perm_scr, y, tm):
    h = F32_SUBLANES
    slabs = y.shape[1] // V7X_LANES
    for j in range(slabs):
        perm_scr[j] = y[:, j * V7X_LANES:(j + 1) * V7X_LANES]
    rows = []
    for a in range(tm // h):
        rows.append(jnp.concatenate(
            [perm_scr[j, pl.ds(_perm_base(a, tm), h, stride=h), :] for j in range(slabs)], axis=1))
    return jnp.concatenate(rows, axis=0)


def _dwconv(hh, w, b, tm):
    h = F32_SUBLANES
    k_w = w.shape[0]
    pad = k_w // 2
    hp, hx, hm = hh[0:h], hh[h:2 * h], hh[2 * h:]
    sub = lax.broadcasted_iota(jnp.int32, hp.shape, 0)
    before = []
    for e in range(pad, 0, -1):
        src = jnp.where(sub == h - 1, pltpu.roll(hp, e - 1, 0) if e > 1 else hp, hm[tm - e * h:tm - (e - 1) * h])
        before.append(pltpu.roll(src, 1, 0))
    after = []
    for e in range(pad):
        src = jnp.where(sub == 0, pltpu.roll(hx, h - e, 0) if e > 0 else hx, hm[e * h:(e + 1) * h])
        after.append(pltpu.roll(src, h - 1, 0))
    ext = jnp.concatenate(before + [hm] + after, axis=0)
    y = b + ext[0:tm] * w[0:1]
    for k in range(1, k_w):
        y = y + ext[k * h:k * h + tm] * w[k:k + 1]
    return y


def _ffn_kernel(xp_ref, x_ref, xn_ref, g_ref, wup_ref, cw_ref, cb_ref, wdn_ref, o_ref, hn_scr,
                perm_scr, act_scr, *, tm, fc, d_ff):
    _fill_normed(hn_scr, perm_scr, xp_ref, x_ref, xn_ref, g_ref[...], tm)
    hn = hn_scr[...]
    for c in range(d_ff // fc):
        lo, hi = c * fc, (c + 1) * fc
        hg = _dot(hn, wup_ref[:, lo:hi])
        hv = _dot(hn, wup_ref[:, d_ff + lo:d_ff + hi])
        gate = _dwconv(hg, cw_ref[:, lo:hi], cb_ref[:, lo:hi], tm)
        val = _dwconv(hv, cw_ref[:, d_ff + lo:d_ff + hi], cb_ref[:, d_ff + lo:d_ff + hi], tm)
        act_scr[:, lo:hi] = (_silu(gate) * val).astype(BF16)
    o_ref[0] = x_ref[0] + _unpermute(perm_scr, _dot(act_scr[...], wdn_ref[...]), tm)


def _conv_ffn(x, g, w_up, conv_w, conv_b, w_down, *, tm=512, fc=256):
    bsz, s_len, d = x.shape
    d_ff = w_down.shape[0]
    tm = min(tm, s_len)
    kern = functools.partial(_ffn_kernel, tm=tm, fc=fc, d_ff=d_ff)
    return pl.pallas_call(
        kern,
        grid=(bsz, s_len // tm),
        in_specs=_halo_specs(tm, s_len, d) + [
            _full((1, d)), _full((d, 2 * d_ff)), _full(conv_w.shape), _full((1, 2 * d_ff)),
            _full((d_ff, d))],
        out_specs=pl.BlockSpec((1, tm, d), lambda b, i: (b, i, 0)),
        out_shape=jax.ShapeDtypeStruct(x.shape, F32),
        scratch_shapes=[pltpu.VMEM((tm + 2 * F32_SUBLANES, d), BF16),
                        pltpu.VMEM((d // V7X_LANES, tm, V7X_LANES), F32),
                        pltpu.VMEM((tm, d_ff), BF16)],
        compiler_params=pltpu.CompilerParams(
            dimension_semantics=("parallel", "parallel"), vmem_limit_bytes=V7X_VMEM_LIMIT),
        name="conv_ffn",
    )(x, x, x, g.reshape(1, d), w_up.astype(BF16), conv_w, conv_b.reshape(1, -1),
      w_down.astype(BF16))


def _rope_tables(s_len):
    half = ROT_DIM // 2
    pos = jnp.arange(s_len, dtype=F32)
    inv_freq = ROPE_THETA ** (-(jnp.arange(0, ROT_DIM, 2, dtype=F32) / ROT_DIM))
    ang = pos[:, None] * inv_freq[None, :]
    cos, sin = jnp.cos(ang), jnp.sin(ang)
    rest = HEAD_DIM - ROT_DIM
    c = jnp.concatenate([cos, cos, jnp.ones((s_len, rest), F32)], axis=1)
    s1 = jnp.concatenate([-sin, jnp.zeros((s_len, half + rest), F32)], axis=1)
    s2 = jnp.concatenate([jnp.zeros((s_len, half), F32), sin, jnp.zeros((s_len, rest), F32)], axis=1)
    rep = V7X_LANES // HEAD_DIM
    return jnp.tile(c, (1, rep)), jnp.tile(s1, (1, rep)), jnp.tile(s2, (1, rep))


def _norm_rope(t, bd, g, c, s1, s2, scale):
    half = ROT_DIM // 2
    msq = _dot((t * t).astype(BF16), bd)
    tn = t * lax.rsqrt(msq + EPS) * g
    outs = []
    for j in range(t.shape[1] // V7X_LANES):
        tc = tn[:, j * V7X_LANES:(j + 1) * V7X_LANES]
        tr = tc * c + pltpu.roll(tc, V7X_LANES - half, 1) * s1 + pltpu.roll(tc, half, 1) * s2
        outs.append(tr * scale if scale != 1.0 else tr)
    return jnp.concatenate(outs, axis=1)


def _qkv_kernel(x_ref, g_ref, w_ref, qg_ref, kg_ref, bdq_ref, bdk_ref, rep_ref, c_ref, s1_ref,
                s2_ref, q_out, k_out, v_out, *, q_dim, kv_dim):
    hn = _rms(x_ref[0], g_ref[...]).astype(BF16)
    c, s1, s2 = c_ref[...], s1_ref[...], s2_ref[...]
    q = _dot(hn, w_ref[:, 0:q_dim])
    q_out[0] = _norm_rope(q, bdq_ref[...], qg_ref[...], c, s1, s2, HEAD_DIM ** -0.5).astype(BF16)
    k = _dot(hn, w_ref[:, q_dim:q_dim + kv_dim])
    kr = _norm_rope(k, bdk_ref[...], kg_ref[...], c, s1, s2, 1.0).astype(BF16)
    k_out[0] = _dot(kr, rep_ref[...]).astype(BF16)
    v = _dot(hn, w_ref[:, q_dim + kv_dim:q_dim + 2 * kv_dim]).astype(BF16)
    v_out[0] = _dot(v, rep_ref[...]).astype(BF16)


def _attn_kernel(sink_ref, q_ref, kp_ref, k_ref, kn_ref, vp_ref, v_ref, vn_ref, x_ref, wo_ref,
                 o_ref, o_scr, *, tq, n_kv, group):
    i = pl.program_id(1)
    n = pl.num_programs(1)
    blk = ATTN_BLOCK
    gw = group * HEAD_DIM
    k_all = jnp.concatenate([kp_ref[0], k_ref[0], kn_ref[0]], axis=0)
    v_all = jnp.concatenate([vp_ref[0], v_ref[0], vn_ref[0]], axis=0)
    row = lax.broadcasted_iota(jnp.int32, (group * blk, 3 * blk), 0) % blk
    col = lax.broadcasted_iota(jnp.int32, (group * blk, 3 * blk), 1)
    band = (col >= row) & (col <= row + 2 * blk)
    head_of_lane = lax.broadcasted_iota(jnp.int32, (blk, gw), 1) // HEAD_DIM
    head_of_row = lax.broadcasted_iota(jnp.int32, (group * blk, 1), 0) // blk
    nblk = tq // blk
    for jb in range(nblk):
        valid = band
        if jb == 0:
            valid = valid & ((col >= blk) | (i > 0))
        if jb == nblk - 1:
            valid = valid & ((col < 2 * blk) | (i < n - 1))
        for kh in range(n_kv):
            qg = q_ref[0, jb * blk:(jb + 1) * blk, kh * gw:(kh + 1) * gw]
            kw = k_all[jb * blk:(jb + 3) * blk, kh * gw:(kh + 1) * gw]
            vw = v_all[jb * blk:(jb + 3) * blk, kh * gw:(kh + 1) * gw]
            qs = jnp.concatenate(
                [jnp.where(head_of_lane == hl, qg, jnp.zeros_like(qg)) for hl in range(group)], axis=0)
            sink = jnp.full((group * blk, 1), sink_ref[kh * group + group - 1], F32)
            for hl in range(group - 2, -1, -1):
                sink = jnp.where(head_of_row == hl, sink_ref[kh * group + hl], sink)
            s = jnp.where(valid, _dot_nt(qs, kw), -1e30)
            m = jnp.maximum(jnp.max(s, axis=-1, keepdims=True), sink)
            p = jnp.exp(s - m)
            denom = jnp.sum(p, axis=-1, keepdims=True) + jnp.exp(sink - m)
            o = _dot(p.astype(BF16), vw) / denom
            og = o[(group - 1) * blk:group * blk]
            for hl in range(group - 2, -1, -1):
                og = jnp.where(head_of_lane == hl, o[hl * blk:(hl + 1) * blk], og)
            o_scr[jb * blk:(jb + 1) * blk, kh * gw:(kh + 1) * gw] = og.astype(BF16)
    o_ref[0] = x_ref[0] + _dot(o_scr[...], wo_ref[...])


def _window_attention(x, norm_g, w_qkv, q_g, k_g, sink, w_o, rope, *, tm=512, tq=512):
    bsz, s_len, d = x.shape
    n_heads = sink.shape[0]
    q_dim = w_o.shape[0]
    kv_dim = (w_qkv.shape[1] - q_dim) // 2
    n_kv = kv_dim // HEAD_DIM
    group = n_heads // n_kv
    tm = min(tm, s_len)
    tq = min(tq, s_len)
    c, s1, s2 = rope

    def block_diag(width):
        idx = jnp.arange(width) // HEAD_DIM
        return jnp.where(idx[:, None] == idx[None, :], 1.0 / HEAD_DIM, 0.0).astype(BF16)

    src = jnp.arange(kv_dim)
    dst = jnp.arange(q_dim)
    rep = ((src[:, None] // HEAD_DIM == dst[None, :] // (group * HEAD_DIM))
           & (src[:, None] % HEAD_DIM == dst[None, :] % HEAD_DIM)).astype(BF16)

    tile = lambda w: pl.BlockSpec((1, tm, w), lambda b, i: (b, i, 0))
    rope_spec = pl.BlockSpec((tm, V7X_LANES), lambda b, i: (i, 0))
    q, k, v = pl.pallas_call(
        functools.partial(_qkv_kernel, q_dim=q_dim, kv_dim=kv_dim),
        grid=(bsz, s_len // tm),
        in_specs=[tile(d), _full((1, d)), _full(w_qkv.shape), _full((1, q_dim)),
                  _full((1, kv_dim)), _full((q_dim, q_dim)), _full((kv_dim, kv_dim)),
                  _full((kv_dim, q_dim)), rope_spec, rope_spec, rope_spec],
        out_specs=[tile(q_dim), tile(q_dim), tile(q_dim)],
        out_shape=[jax.ShapeDtypeStruct((bsz, s_len, q_dim), BF16)] * 3,
        compiler_params=pltpu.CompilerParams(
            dimension_semantics=("parallel", "parallel"), vmem_limit_bytes=V7X_VMEM_LIMIT),
        name="attn_qkv",
    )(x, norm_g.reshape(1, d), w_qkv.astype(BF16), jnp.tile(q_g, n_heads).reshape(1, q_dim),
      jnp.tile(k_g, n_kv).reshape(1, kv_dim), block_diag(q_dim), block_diag(kv_dim), rep, c, s1, s2)

    r = tq // ATTN_BLOCK
    last = s_len // ATTN_BLOCK - 1
    prev_spec = pl.BlockSpec((1, ATTN_BLOCK, q_dim), lambda b, i: (b, jnp.maximum(i * r - 1, 0), 0))
    main_spec = pl.BlockSpec((1, tq, q_dim), lambda b, i: (b, i, 0))
    next_spec = pl.BlockSpec((1, ATTN_BLOCK, q_dim), lambda b, i: (b, jnp.minimum((i + 1) * r, last), 0))
    return pl.pallas_call(
        functools.partial(_attn_kernel, tq=tq, n_kv=n_kv, group=group),
        grid=(bsz, s_len // tq),
        in_specs=[pl.BlockSpec(memory_space=pltpu.SMEM), main_spec,
                  prev_spec, main_spec, next_spec, prev_spec, main_spec, next_spec,
                  pl.BlockSpec((1, tq, d), lambda b, i: (b, i, 0)), _full(w_o.shape)],
        out_specs=pl.BlockSpec((1, tq, d), lambda b, i: (b, i, 0)),
        out_shape=jax.ShapeDtypeStruct(x.shape, F32),
        scratch_shapes=[pltpu.VMEM((tq, q_dim), BF16)],
        compiler_params=pltpu.CompilerParams(
            dimension_semantics=("parallel", "parallel"), vmem_limit_bytes=V7X_VMEM_LIMIT),
        name="attn_core",
    )(sink.astype(F32), q, k, k, k, v, v, v, x, w_o.astype(BF16))


def _softplus(x):
    return jnp.maximum(x, 0.0) + jnp.log1p(jnp.exp(-jnp.abs(x)))


def _ssd_in_kernel(xp_ref, x_ref, xn_ref, g_ref, w_ref, wdt_ref, cw_ref, cb_ref, dtb_ref,
                   z_out, xbc_out, dt_out, hn_scr, perm_scr, *, tm, fc, d_inner, conv_dim):
    g = g_ref[...]
    _fill_normed(hn_scr, perm_scr, xp_ref, x_ref, xn_ref, g, tm)
    hn_nat = _rms(x_ref[0], g).astype(BF16)
    z_out[0] = _silu(_dot(hn_nat, w_ref[:, 0:d_inner])).astype(BF16)
    dt_out[0] = _softplus(_dot(hn_nat, wdt_ref[...]) + dtb_ref[...])
    hn = hn_scr[...]
    for c in range(conv_dim // fc):
        lo, hi = c * fc, (c + 1) * fc
        h = _dot(hn, w_ref[:, d_inner + lo:d_inner + hi])
        y = _unpermute(perm_scr, _dwconv(h, cw_ref[:, lo:hi], cb_ref[:, lo:hi], tm), tm)
        xbc_out[0, :, lo:hi] = _silu(y).astype(BF16)


def _pieces(x, n):
    out = []
    for _ in range(n - 1):
        p = x.astype(BF16).astype(F32)
        out.append(p)
        x = x - p
    out.append(x.astype(BF16).astype(F32))
    return out


def _ssd_chunk(xs, bm, cm, dt_all, a_row, a_col, esel, xsel, state_ref, reverse, off, heads):
    n_chunk, d_inner = xs.shape
    hpg = heads // SSD_GROUPS
    gw = hpg * SSD_HEAD_DIM
    dt = dt_all[:, off:off + heads]
    dtt = dt_all.T[off:off + heads, :]
    row = lax.broadcasted_iota(jnp.int32, (n_chunk, n_chunk), 0)
    col = lax.broadcasted_iota(jnp.int32, (n_chunk, n_chunk), 1)
    later = (row <= col) if reverse else (row >= col)
    tri = later.astype(BF16)
    tri_t = ((row >= col) if reverse else (row <= col)).astype(BF16)
    p1, p2, p3 = [p.astype(BF16) for p in _pieces(dt * a_row, 3)]
    a_cum = (_dot(tri, p1) + _dot(tri, p2)) + _dot(tri, p3)
    q1, q2, q3 = [q.astype(BF16) for q in _pieces(dtt * a_col, 3)]
    a_cum_t = (_dot(q1, tri_t) + _dot(q2, tri_t)) + _dot(q3, tri_t)
    end = 0 if reverse else n_chunk - 1
    a_end = a_cum[end:end + 1, :]
    e_cum = jnp.exp(a_cum)
    w_in = dt * jnp.exp(a_end - a_cum)
    acol = _dot(jnp.concatenate(_pieces(a_cum, 3), axis=1).astype(BF16), esel)
    ew = _dot(jnp.concatenate(_pieces(e_cum, 2) + _pieces(w_in, 2), axis=1).astype(BF16), xsel)
    e_exp, w_exp = ew[:, 0:d_inner], ew[:, d_inner:2 * d_inner]
    lane_head = lax.broadcasted_iota(jnp.int32, (n_chunk, gw), 1) // SSD_HEAD_DIM
    ys = []
    for g in range(SSD_GROUPS):
        b_g = bm[:, g * D_STATE:(g + 1) * D_STATE]
        c_g = cm[:, g * D_STATE:(g + 1) * D_STATE]
        x_g = xs[:, g * gw:(g + 1) * gw]
        cb = _dot_nt(c_g, b_g)
        ms, xb = [], []
        for r in range(hpg):
            h = g * hpg + r
            diff = acol[:, h * n_chunk:(h + 1) * n_chunk] - a_cum_t[h:h + 1, :]
            decay = jnp.exp(jnp.where(later, diff, -1e30))
            ms.append((cb * decay * dtt[h:h + 1, :]).astype(BF16))
            xb.append(jnp.where(lane_head == r, x_g, jnp.zeros_like(x_g)))
        y_diag = _dot(jnp.concatenate(ms, axis=1), jnp.concatenate(xb, axis=0))
        st = state_ref[g]
        ys.append(y_diag + _dot(c_g, st.astype(BF16)) * e_exp[:, g * gw:(g + 1) * gw])
        xw = (x_g.astype(F32) * w_exp[:, g * gw:(g + 1) * gw]).astype(BF16)
        state_ref[g] = e_exp[end:end + 1, g * gw:(g + 1) * gw] * st + _dot_tn(b_g, xw)
    return jnp.concatenate(ys, axis=1)


def _ssd_bwd_kernel(xbc_ref, dt_ref, a_ref, at_ref, esel_ref, xsel_ref, y_out, state_ref,
                    *, d_inner, heads):
    @pl.when(pl.program_id(1) == 0)
    def _():
        state_ref[...] = jnp.zeros_like(state_ref)

    gn = SSD_GROUPS * D_STATE
    xbc = xbc_ref[0]
    y_out[0] = _ssd_chunk(xbc[:, 0:d_inner], xbc[:, d_inner:d_inner + gn], xbc[:, d_inner + gn:],
                          dt_ref[0], a_ref[1:2, :], at_ref[:, 1:2],
                          esel_ref[...], xsel_ref[...], state_ref, True, heads, heads)


def _ssd_fwd_kernel(xbc_ref, dt_ref, a_ref, at_ref, esel_ref, xsel_ref, yb_ref, z_ref, x_ref,
                    dexp_ref, gg_ref, wout_ref, o_ref, state_ref, yn_scr, *, d_inner, heads):
    @pl.when(pl.program_id(1) == 0)
    def _():
        state_ref[...] = jnp.zeros_like(state_ref)

    gn = SSD_GROUPS * D_STATE
    xbc = xbc_ref[0]
    xs = xbc[:, 0:d_inner]
    y = _ssd_chunk(xs, xbc[:, d_inner:d_inner + gn], xbc[:, d_inner + gn:],
                   dt_ref[0], a_ref[0:1, :], at_ref[:, 0:1],
                   esel_ref[...], xsel_ref[...], state_ref, False, 0, heads)
    y = y + yb_ref[0] + xs.astype(F32) * dexp_ref[...]
    y = y * z_ref[0].astype(F32)
    gw = d_inner // SSD_GROUPS
    for g in range(SSD_GROUPS):
        yg = y[:, g * gw:(g + 1) * gw]
        yg = yg * lax.rsqrt(jnp.mean(yg * yg, axis=-1, keepdims=True) + EPS)
        yn_scr[:, g * gw:(g + 1) * gw] = (yg * gg_ref[:, g * gw:(g + 1) * gw]).astype(BF16)
    o_ref[0] = x_ref[0] + _dot(yn_scr[...], wout_ref[...])


def _ssd_mixer(x, norm_g, w_in, conv_w, conv_b, dt_bias, a_log, d_skip, gate_g, w_out,
               *, tm=512, fc=512):
    bsz, s_len, d = x.shape
    d_inner = w_out.shape[0]
    heads = d_skip.shape[0]
    conv_dim = conv_w.shape[1]
    tm = min(tm, s_len)
    w_bf = w_in.astype(BF16)
    dt_pad = V7X_LANES - 2 * heads
    w_dt = jnp.pad(w_bf[:, d_inner + conv_dim:], ((0, 0), (0, dt_pad)))
    dt_b = jnp.pad(dt_bias.reshape(1, 2 * heads).astype(F32), ((0, 0), (0, dt_pad)))
    tile = lambda w: pl.BlockSpec((1, tm, w), lambda b, i: (b, i, 0))
    sz, xbc, dt = pl.pallas_call(
        functools.partial(_ssd_in_kernel, tm=tm, fc=fc, d_inner=d_inner, conv_dim=conv_dim),
        grid=(bsz, s_len // tm),
        in_specs=_halo_specs(tm, s_len, d) + [
            _full((1, d)), _full(w_in.shape), _full((d, V7X_LANES)), _full(conv_w.shape),
            _full((1, conv_dim)), _full((1, V7X_LANES))],
        out_specs=[tile(d_inner), tile(conv_dim), tile(V7X_LANES)],
        out_shape=[jax.ShapeDtypeStruct((bsz, s_len, d_inner), BF16),
                   jax.ShapeDtypeStruct((bsz, s_len, conv_dim), BF16),
                   jax.ShapeDtypeStruct((bsz, s_len, V7X_LANES), F32)],
        scratch_shapes=[pltpu.VMEM((tm + 2 * F32_SUBLANES, d), BF16),
                        pltpu.VMEM((d // V7X_LANES, tm, V7X_LANES), F32)],
        compiler_params=pltpu.CompilerParams(
            dimension_semantics=("parallel", "parallel"), vmem_limit_bytes=V7X_VMEM_LIMIT),
        name="ssd_in",
    )(x, x, x, norm_g.reshape(1, d), w_bf, w_dt, conv_w, conv_b.reshape(1, conv_dim), dt_b)

    a_diag = -jnp.exp(a_log.astype(F32))
    lc = SSD_CHUNK
    nc = s_len // lc
    hpg = heads // SSD_GROUPS
    er = lax.broadcasted_iota(jnp.int32, (3 * heads, heads * lc), 0)
    ec = lax.broadcasted_iota(jnp.int32, (3 * heads, heads * lc), 1)
    esel = (er % heads == ec // lc).astype(BF16)
    xr = lax.broadcasted_iota(jnp.int32, (4 * heads, 2 * d_inner), 0)
    xc = lax.broadcasted_iota(jnp.int32, (4 * heads, 2 * d_inner), 1)
    xsel = ((xr % heads == (xc % d_inner) // SSD_HEAD_DIM)
            & (xr // (2 * heads) == xc // d_inner)).astype(BF16)
    state = pltpu.VMEM((SSD_GROUPS, D_STATE, hpg * SSD_HEAD_DIM), F32)
    rev = lambda w: pl.BlockSpec((1, lc, w), lambda b, c: (b, nc - 1 - c, 0))
    fwd = lambda w: pl.BlockSpec((1, lc, w), lambda b, c: (b, c, 0))
    y_b = pl.pallas_call(
        functools.partial(_ssd_bwd_kernel, d_inner=d_inner, heads=heads),
        grid=(bsz, nc),
        in_specs=[rev(conv_dim), rev(V7X_LANES), _full((2, heads)), _full((heads, 2)),
                  _full(esel.shape), _full(xsel.shape)],
        out_specs=rev(d_inner),
        out_shape=jax.ShapeDtypeStruct((bsz, s_len, d_inner), F32),
        scratch_shapes=[state],
        compiler_params=pltpu.CompilerParams(
            dimension_semantics=("parallel", "arbitrary"), vmem_limit_bytes=V7X_VMEM_LIMIT),
        name="ssd_scan_bwd",
    )(xbc, dt, a_diag, a_diag.T, esel, xsel)

    return pl.pallas_call(
        functools.partial(_ssd_fwd_kernel, d_inner=d_inner, heads=heads),
        grid=(bsz, nc),
        in_specs=[fwd(conv_dim), fwd(V7X_LANES), _full((2, heads)), _full((heads, 2)),
                  _full(esel.shape), _full(xsel.shape), fwd(d_inner), fwd(d_inner), fwd(d),
                  _full((1, d_inner)), _full((1, d_inner)), _full(w_out.shape)],
        out_specs=fwd(d),
        out_shape=jax.ShapeDtypeStruct(x.shape, F32),
        scratch_shapes=[state, pltpu.VMEM((lc, d_inner), BF16)],
        compiler_params=pltpu.CompilerParams(
            dimension_semantics=("parallel", "arbitrary"), vmem_limit_bytes=V7X_VMEM_LIMIT),
        name="ssd_scan_fwd",
    )(xbc, dt, a_diag, a_diag.T, esel, xsel, y_b, sz, x,
      jnp.broadcast_to(d_skip.astype(F32)[:, None], (heads, SSD_HEAD_DIM)).reshape(1, d_inner),
      gate_g.reshape(1, d_inner).astype(F32), w_out.astype(BF16))


def kernel(x, attn_norm, attn_w_qkv, attn_q_norm, attn_k_norm, attn_sink, attn_w_o, ssd_norm, ssd_w_in, ssd_conv_w, ssd_conv_b, ssd_dt_bias, ssd_a_log, ssd_d, ssd_gate_norm, ssd_w_out, ffn_norm, ffn_w_up, ffn_conv_w, ffn_conv_b, ffn_w_down):
    depth = ffn_norm.shape[0]
    rope = _rope_tables(x.shape[1])
    for i in range(depth):
        j = i // 2
        if i % 2 == 0:
            x = _window_attention(x, attn_norm[j], attn_w_qkv[j], attn_q_norm[j], attn_k_norm[j],
                                  attn_sink[j], attn_w_o[j], rope)
        else:
            x = _ssd_mixer(x, ssd_norm[j], ssd_w_in[j], ssd_conv_w[j], ssd_conv_b[j],
                           ssd_dt_bias[j], ssd_a_log[j], ssd_d[j], ssd_gate_norm[j], ssd_w_out[j])
        x = _conv_ffn(x, ffn_norm[i], ffn_w_up[i], ffn_conv_w[i], ffn_conv_b[i], ffn_w_down[i])
    return x
```

```python
import functools

import jax
import jax.numpy as jnp
from jax import lax
from jax.experimental import pallas as pl
from jax.experimental.pallas import tpu as pltpu

F32 = jnp.float32
BF16 = jnp.bfloat16

EPS = 1e-6
HEAD_DIM = 64
ROT_DIM = HEAD_DIM // 4
ROPE_THETA = 500000.0
ATTN_BLOCK = 128
SSD_HEAD_DIM = 64
SSD_GROUPS = 8
D_STATE = 128
SSD_CHUNK = 128

V7X_LANES = 128
F32_SUBLANES = 8
V7X_VMEM_LIMIT = 56 * 1024 * 1024


def _rms(x, g):
    return x * lax.rsqrt(jnp.mean(x * x, axis=-1, keepdims=True) + EPS) * g


def _silu(x):
    h = 0.5 * x
    return h + h * jnp.tanh(h)


def _dot(a, b):
    return jnp.dot(a, b, preferred_element_type=F32)


def _dot_nt(a, b):
    return lax.dot_general(a, b, (((1,), (1,)), ((), ())), preferred_element_type=F32)


def _dot_tn(a, b):
    return lax.dot_general(a, b, (((0,), (0,)), ((), ())), preferred_element_type=F32)


def _halo_specs(tm, s_len, d):
    r = tm // F32_SUBLANES
    last = s_len // F32_SUBLANES - 1
    return [
        pl.BlockSpec((1, F32_SUBLANES, d), lambda b, i: (b, jnp.maximum(i * r - 1, 0), 0)),
        pl.BlockSpec((1, tm, d), lambda b, i: (b, i, 0)),
        pl.BlockSpec((1, F32_SUBLANES, d), lambda b, i: (b, jnp.minimum((i + 1) * r, last), 0)),
    ]


def _full(shape):
    return pl.BlockSpec(shape, lambda b, i: (0,) * len(shape), pipeline_mode=pl.Buffered(1))


def _perm_base(a, tm):
    p = tm // F32_SUBLANES
    t = a * F32_SUBLANES
    return (t % p) * F32_SUBLANES + t // p


def _fill_normed(hn_scr, perm_scr, xp_ref, x_ref, xn_ref, g, tm):
    i = pl.program_id(1)
    n = pl.num_programs(1)
    h = F32_SUBLANES
    keep_p = (i > 0).astype(F32)
    keep_n = (i < n - 1).astype(F32)
    halo = jnp.concatenate([_rms(xp_ref[0], g) * keep_p, _rms(xn_ref[0], g) * keep_n], axis=0)
    hn_scr[0:2 * h, :] = halo.astype(BF16)
    xn = _rms(x_ref[0], g)
    slabs = xn.shape[1] // V7X_LANES
    for a in range(tm // h):
        for j in range(slabs):
            perm_scr[j, pl.ds(_perm_base(a, tm), h, stride=h), :] = (
                xn[a * h:(a + 1) * h, j * V7X_LANES:(j + 1) * V7X_LANES])
    for j in range(slabs):
        hn_scr[2 * h:2 * h + tm, j * V7X_LANES:(j + 1) * V7X_LANES] = perm_scr[j].astype(BF16)


def _unpermute(perm_scr, y, tm):
    h = F32_SUBLANES
    slabs = y.shape[1] // V7X_LANES
    for j in range(slabs):
        perm_scr[j] = y[:, j * V7X_LANES:(j + 1) * V7X_LANES]
    rows = []
    for a in range(tm // h):
        rows.append(jnp.concatenate(
            [perm_scr[j, pl.ds(_perm_base(a, tm), h, stride=h), :] for j in range(slabs)], axis=1))
    return jnp.concatenate(rows, axis=0)


def _dwconv(hh, w, b, tm):
    h = F32_SUBLANES
    k_w = w.shape[0]
    pad = k_w // 2
    hp, hx, hm = hh[0:h], hh[h:2 * h], hh[2 * h:]
    sub = lax.broadcasted_iota(jnp.int32, hp.shape, 0)
    before = []
    for e in range(pad, 0, -1):
        src = jnp.where(sub == h - 1, pltpu.roll(hp, e - 1, 0) if e > 1 else hp, hm[tm - e * h:tm - (e - 1) * h])
        before.append(pltpu.roll(src, 1, 0))
    after = []
    for e in range(pad):
        src = jnp.where(sub == 0, pltpu.roll(hx, h - e, 0) if e > 0 else hx, hm[e * h:(e + 1) * h])
        after.append(pltpu.roll(src, h - 1, 0))
    ext = jnp.concatenate(before + [hm] + after, axis=0)
    y = b + ext[0:tm] * w[0:1]
    for k in range(1, k_w):
        y = y + ext[k * h:k * h + tm] * w[k:k + 1]
    return y


def _ffn_kernel(xp_ref, x_ref, xn_ref, g_ref, wup_ref, cw_ref, cb_ref, wdn_ref, o_ref, hn_scr,
                perm_scr, act_scr, *, tm, fc, d_ff):
    _fill_normed(hn_scr, perm_scr, xp_ref, x_ref, xn_ref, g_ref[...], tm)
    hn = hn_scr[...]
    for c in range(d_ff // fc):
        lo, hi = c * fc, (c + 1) * fc
        hg = _dot(hn, wup_ref[:, lo:hi])
        hv = _dot(hn, wup_ref[:, d_ff + lo:d_ff + hi])
        gate = _dwconv(hg, cw_ref[:, lo:hi], cb_ref[:, lo:hi], tm)
        val = _dwconv(hv, cw_ref[:, d_ff + lo:d_ff + hi], cb_ref[:, d_ff + lo:d_ff + hi], tm)
        act_scr[:, lo:hi] = (_silu(gate) * val).astype(BF16)
    o_ref[0] = x_ref[0] + _unpermute(perm_scr, _dot(act_scr[...], wdn_ref[...]), tm)


def _conv_ffn(x, g, w_up, conv_w, conv_b, w_down, *, tm=1024, fc=256):
    bsz, s_len, d = x.shape
    d_ff = w_down.shape[0]
    tm = min(tm, s_len)
    kern = functools.partial(_ffn_kernel, tm=tm, fc=fc, d_ff=d_ff)
    return pl.pallas_call(
        kern,
        grid=(bsz, s_len // tm),
        in_specs=_halo_specs(tm, s_len, d) + [
            _full((1, d)), _full((d, 2 * d_ff)), _full(conv_w.shape), _full((1, 2 * d_ff)),
            _full((d_ff, d))],
        out_specs=pl.BlockSpec((1, tm, d), lambda b, i: (b, i, 0)),
        out_shape=jax.ShapeDtypeStruct(x.shape, F32),
        scratch_shapes=[pltpu.VMEM((tm + 2 * F32_SUBLANES, d), BF16),
                        pltpu.VMEM((d // V7X_LANES, tm, V7X_LANES), F32),
                        pltpu.VMEM((tm, d_ff), BF16)],
        compiler_params=pltpu.CompilerParams(
            dimension_semantics=("parallel", "parallel"), vmem_limit_bytes=V7X_VMEM_LIMIT),
        name="conv_ffn",
    )(x, x, x, g.reshape(1, d), w_up.astype(BF16), conv_w, conv_b.reshape(1, -1),
      w_down.astype(BF16))


def _rope_tables(s_len):
    half = ROT_DIM // 2
    pos = jnp.arange(s_len, dtype=F32)
    inv_freq = ROPE_THETA ** (-(jnp.arange(0, ROT_DIM, 2, dtype=F32) / ROT_DIM))
    ang = pos[:, None] * inv_freq[None, :]
    cos, sin = jnp.cos(ang), jnp.sin(ang)
    rest = HEAD_DIM - ROT_DIM
    c = jnp.concatenate([cos, cos, jnp.ones((s_len, rest), F32)], axis=1)
    s1 = jnp.concatenate([-sin, jnp.zeros((s_len, half + rest), F32)], axis=1)
    s2 = jnp.concatenate([jnp.zeros((s_len, half), F32), sin, jnp.zeros((s_len, rest), F32)], axis=1)
    rep = V7X_LANES // HEAD_DIM
    return jnp.tile(c, (1, rep)), jnp.tile(s1, (1, rep)), jnp.tile(s2, (1, rep))


def _norm_rope(t, bd, g, c, s1, s2, scale):
    half = ROT_DIM // 2
    msq = _dot((t * t).astype(BF16), bd)
    tn = t * lax.rsqrt(msq + EPS) * g
    outs = []
    for j in range(t.shape[1] // V7X_LANES):
        tc = tn[:, j * V7X_LANES:(j + 1) * V7X_LANES]
        tr = tc * c + pltpu.roll(tc, V7X_LANES - half, 1) * s1 + pltpu.roll(tc, half, 1) * s2
        outs.append(tr * scale if scale != 1.0 else tr)
    return jnp.concatenate(outs, axis=1)


def _qkv_kernel(x_ref, g_ref, w_ref, qg_ref, kg_ref, bdq_ref, bdk_ref, rep_ref, c_ref, s1_ref,
                s2_ref, q_out, k_out, v_out, *, q_dim, kv_dim):
    hn = _rms(x_ref[0], g_ref[...]).astype(BF16)
    c, s1, s2 = c_ref[...], s1_ref[...], s2_ref[...]
    q = _dot(hn, w_ref[:, 0:q_dim])
    q_out[0] = _norm_rope(q, bdq_ref[...], qg_ref[...], c, s1, s2, HEAD_DIM ** -0.5).astype(BF16)
    k = _dot(hn, w_ref[:, q_dim:q_dim + kv_dim])
    kr = _norm_rope(k, bdk_ref[...], kg_ref[...], c, s1, s2, 1.0).astype(BF16)
    k_out[0] = _dot(kr, rep_ref[...]).astype(BF16)
    v = _dot(hn, w_ref[:, q_dim + kv_dim:q_dim + 2 * kv_dim]).astype(BF16)
    v_out[0] = _dot(v, rep_ref[...]).astype(BF16)


def _attn_kernel(sink_ref, q_ref, kp_ref, k_ref, kn_ref, vp_ref, v_ref, vn_ref, x_ref, wo_ref,
                 o_ref, o_scr, *, tq, n_kv, group):
    i = pl.program_id(1)
    n = pl.num_programs(1)
    blk = ATTN_BLOCK
    gw = group * HEAD_DIM
    k_all = jnp.concatenate([kp_ref[0], k_ref[0], kn_ref[0]], axis=0)
    v_all = jnp.concatenate([vp_ref[0], v_ref[0], vn_ref[0]], axis=0)
    row = lax.broadcasted_iota(jnp.int32, (group * blk, 3 * blk), 0) % blk
    col = lax.broadcasted_iota(jnp.int32, (group * blk, 3 * blk), 1)
    band = (col >= row) & (col <= row + 2 * blk)
    head_of_lane = lax.broadcasted_iota(jnp.int32, (blk, gw), 1) // HEAD_DIM
    head_of_row = lax.broadcasted_iota(jnp.int32, (group * blk, 1), 0) // blk
    nblk = tq // blk
    for jb in range(nblk):
        valid = band
        if jb == 0:
            valid = valid & ((col >= blk) | (i > 0))
        if jb == nblk - 1:
            valid = valid & ((col < 2 * blk) | (i < n - 1))
        for kh in range(n_kv):
            qg = q_ref[0, jb * blk:(jb + 1) * blk, kh * gw:(kh + 1) * gw]
            kw = k_all[jb * blk:(jb + 3) * blk, kh * gw:(kh + 1) * gw]
            vw = v_all[jb * blk:(jb + 3) * blk, kh * gw:(kh + 1) * gw]
            qs = jnp.concatenate(
                [jnp.where(head_of_lane == hl, qg, jnp.zeros_like(qg)) for hl in range(group)], axis=0)
            sink = jnp.full((group * blk, 1), sink_ref[kh * group + group - 1], F32)
            for hl in range(group - 2, -1, -1):
                sink = jnp.where(head_of_row == hl, sink_ref[kh * group + hl], sink)
            s = jnp.where(valid, _dot_nt(qs, kw), -1e30)
            m = jnp.maximum(jnp.max(s, axis=-1, keepdims=True), sink)
            p = jnp.exp(s - m)
            denom = jnp.sum(p, axis=-1, keepdims=True) + jnp.exp(sink - m)
            o = _dot(p.astype(BF16), vw) / denom
            og = o[(group - 1) * blk:group * blk]
            for hl in range(group - 2, -1, -1):
                og = jnp.where(head_of_lane == hl, o[hl * blk:(hl + 1) * blk], og)
            o_scr[jb * blk:(jb + 1) * blk, kh * gw:(kh + 1) * gw] = og.astype(BF16)
    o_ref[0] = x_ref[0] + _dot(o_scr[...], wo_ref[...])


def _window_attention(x, norm_g, w_qkv, q_g, k_g, sink, w_o, rope, *, tm=512, tq=512):
    bsz, s_len, d = x.shape
    n_heads = sink.shape[0]
    q_dim = w_o.shape[0]
    kv_dim = (w_qkv.shape[1] - q_dim) // 2
    n_kv = kv_dim // HEAD_DIM
    group = n_heads // n_kv
    tm = min(tm, s_len)
    tq = min(tq, s_len)
    c, s1, s2 = rope

    def block_diag(width):
        idx = jnp.arange(width) // HEAD_DIM
        return jnp.where(idx[:, None] == idx[None, :], 1.0 / HEAD_DIM, 0.0).astype(BF16)

    src = jnp.arange(kv_dim)
    dst = jnp.arange(q_dim)
    rep = ((src[:, None] // HEAD_DIM == dst[None, :] // (group * HEAD_DIM))
           & (src[:, None] % HEAD_DIM == dst[None, :] % HEAD_DIM)).astype(BF16)

    tile = lambda w: pl.BlockSpec((1, tm, w), lambda b, i: (b, i, 0))
    rope_spec = pl.BlockSpec((tm, V7X_LANES), lambda b, i: (i, 0))
    q, k, v = pl.pallas_call(
        functools.partial(_qkv_kernel, q_dim=q_dim, kv_dim=kv_dim),
        grid=(bsz, s_len // tm),
        in_specs=[tile(d), _full((1, d)), _full(w_qkv.shape), _full((1, q_dim)),
                  _full((1, kv_dim)), _full((q_dim, q_dim)), _full((kv_dim, kv_dim)),
                  _full((kv_dim, q_dim)), rope_spec, rope_spec, rope_spec],
        out_specs=[tile(q_dim), tile(q_dim), tile(q_dim)],
        out_shape=[jax.ShapeDtypeStruct((bsz, s_len, q_dim), BF16)] * 3,
        compiler_params=pltpu.CompilerParams(
            dimension_semantics=("parallel", "parallel"), vmem_limit_bytes=V7X_VMEM_LIMIT),
        name="attn_qkv",
    )(x, norm_g.reshape(1, d), w_qkv.astype(BF16), jnp.tile(q_g, n_heads).reshape(1, q_dim),
      jnp.tile(k_g, n_kv).reshape(1, kv_dim), block_diag(q_dim), block_diag(kv_dim), rep, c, s1, s2)

    r = tq // ATTN_BLOCK
    last = s_len // ATTN_BLOCK - 1
    prev_spec = pl.BlockSpec((1, ATTN_BLOCK, q_dim), lambda b, i: (b, jnp.maximum(i * r - 1, 0), 0))
    main_spec = pl.BlockSpec((1, tq, q_dim), lambda b, i: (b, i, 0))
    next_spec = pl.BlockSpec((1, ATTN_BLOCK, q_dim), lambda b, i: (b, jnp.minimum((i + 1) * r, last), 0))
    return pl.pallas_call(
        functools.partial(_attn_kernel, tq=tq, n_kv=n_kv, group=group),
        grid=(bsz, s_len // tq),
        in_specs=[pl.BlockSpec(memory_space=pltpu.SMEM), main_spec,
                  prev_spec, main_spec, next_spec, prev_spec, main_spec, next_spec,
                  pl.BlockSpec((1, tq, d), lambda b, i: (b, i, 0)), _full(w_o.shape)],
        out_specs=pl.BlockSpec((1, tq, d), lambda b, i: (b, i, 0)),
        out_shape=jax.ShapeDtypeStruct(x.shape, F32),
        scratch_shapes=[pltpu.VMEM((tq, q_dim), BF16)],
        compiler_params=pltpu.CompilerParams(
            dimension_semantics=("parallel", "parallel"), vmem_limit_bytes=V7X_VMEM_LIMIT),
        name="attn_core",
    )(sink.astype(F32), q, k, k, k, v, v, v, x, w_o.astype(BF16))


def _softplus(x):
    return jnp.maximum(x, 0.0) + jnp.log1p(jnp.exp(-jnp.abs(x)))


def _ssd_in_kernel(xp_ref, x_ref, xn_ref, g_ref, w_ref, wdt_ref, cw_ref, cb_ref, dtb_ref,
                   z_out, xbc_out, dt_out, hn_scr, perm_scr, *, tm, fc, d_inner, conv_dim):
    g = g_ref[...]
    _fill_normed(hn_scr, perm_scr, xp_ref, x_ref, xn_ref, g, tm)
    hn_nat = _rms(x_ref[0], g).astype(BF16)
    z_out[0] = _silu(_dot(hn_nat, w_ref[:, 0:d_inner])).astype(BF16)
    dt_out[0] = _softplus(_dot(hn_nat, wdt_ref[...]) + dtb_ref[...])
    hn = hn_scr[...]
    for c in range(conv_dim // fc):
        lo, hi = c * fc, (c + 1) * fc
        h = _dot(hn, w_ref[:, d_inner + lo:d_inner + hi])
        y = _unpermute(perm_scr, _dwconv(h, cw_ref[:, lo:hi], cb_ref[:, lo:hi], tm), tm)
        xbc_out[0, :, lo:hi] = _silu(y).astype(BF16)


def _pieces(x, n):
    out = []
    for _ in range(n - 1):
        p = x.astype(BF16).astype(F32)
        out.append(p)
        x = x - p
    out.append(x.astype(BF16).astype(F32))
    return out


def _ssd_decay_kernel(dt_ref, a_ref, at_ref, acum_out, tr_out, ew_out, *, heads, cps):
    lc = SSD_CHUNK
    row = lax.broadcasted_iota(jnp.int32, (lc, lc), 0)
    col = lax.broadcasted_iota(jnp.int32, (lc, lc), 1)
    lower = (row >= col).astype(BF16)
    upper = (row <= col).astype(BF16)
    fwd_lane = lax.broadcasted_iota(jnp.int32, (lc, V7X_LANES), 1) < heads
    fwd_row = lax.broadcasted_iota(jnp.int32, (V7X_LANES, lc), 0) < heads
    for k in range(cps):
        rows = slice(k * lc, (k + 1) * lc)
        dt = dt_ref[0, rows, :]
        dtt = dt.T
        ps = [p.astype(BF16) for p in _pieces(dt * a_ref[...], 3)]
        cum_f = (_dot(lower, ps[0]) + _dot(lower, ps[1])) + _dot(lower, ps[2])
        cum_b = (_dot(upper, ps[0]) + _dot(upper, ps[1])) + _dot(upper, ps[2])
        a_cum = jnp.where(fwd_lane, cum_f, cum_b)
        qs = [q.astype(BF16) for q in _pieces(dtt * at_ref[...], 3)]
        cum_tf = (_dot(qs[0], upper) + _dot(qs[1], upper)) + _dot(qs[2], upper)
        cum_tb = (_dot(qs[0], lower) + _dot(qs[1], lower)) + _dot(qs[2], lower)
        a_cum_t = jnp.where(fwd_row, cum_tf, cum_tb)
        a_end = jnp.where(fwd_lane[0:1], a_cum[lc - 1:lc, :], a_cum[0:1, :])
        e1, e2 = _pieces(jnp.exp(a_cum), 2)
        w1, w2 = _pieces(dt * jnp.exp(a_end - a_cum), 2)
        acum_out[0, rows, :] = a_cum
        half = V7X_LANES // 2
        tr_out[0, :, rows] = jnp.concatenate([a_cum_t[0:half], dtt[0:half]], axis=0)
        for d in range(2):
            lanes = slice(d * heads, (d + 1) * heads)
            ew_out[0, rows, d * V7X_LANES:(d + 1) * V7X_LANES] = jnp.concatenate(
                [e1[:, lanes], e2[:, lanes], w1[:, lanes], w2[:, lanes]], axis=1).astype(BF16)


def _ssd_chunk(xs, bm, cm, acum, tr, ewl, xsel, state_ref, reverse, off, heads):
    n_chunk, d_inner = xs.shape
    hpg = heads // SSD_GROUPS
    gw = hpg * SSD_HEAD_DIM
    dtt_off = V7X_LANES // 2 + off
    row = lax.broadcasted_iota(jnp.int32, (n_chunk, n_chunk), 0)
    col = lax.broadcasted_iota(jnp.int32, (n_chunk, n_chunk), 1)
    later = (row <= col) if reverse else (row >= col)
    end = 0 if reverse else n_chunk - 1
    ew = _dot(ewl, xsel)
    e_exp, w_exp = ew[:, 0:d_inner], ew[:, d_inner:2 * d_inner]
    lane_head = lax.broadcasted_iota(jnp.int32, (n_chunk, gw), 1) // SSD_HEAD_DIM
    ys = []
    for g in range(SSD_GROUPS):
        b_g = bm[:, g * D_STATE:(g + 1) * D_STATE]
        c_g = cm[:, g * D_STATE:(g + 1) * D_STATE]
        x_g = xs[:, g * gw:(g + 1) * gw]
        cb = _dot_nt(c_g, b_g)
        ms, xb = [], []
        for r in range(hpg):
            h = g * hpg + r
            diff = acum[:, off + h:off + h + 1] - tr[off + h:off + h + 1, :]
            decay = jnp.exp(jnp.where(later, diff, -1e30))
            ms.append((cb * decay * tr[dtt_off + h:dtt_off + h + 1, :]).astype(BF16))
            xb.append(jnp.where(lane_head == r, x_g, jnp.zeros_like(x_g)))
        y_diag = _dot(jnp.concatenate(ms, axis=1), jnp.concatenate(xb, axis=0))
        st = state_ref[g]
        ys.append(y_diag + _dot(c_g, st.astype(BF16)) * e_exp[:, g * gw:(g + 1) * gw])
        xw = (x_g.astype(F32) * w_exp[:, g * gw:(g + 1) * gw]).astype(BF16)
        state_ref[g] = e_exp[end:end + 1, g * gw:(g + 1) * gw] * st + _dot_tn(b_g, xw)
    return jnp.concatenate(ys, axis=1)


def _ssd_bwd_kernel(xbc_ref, acum_ref, tr_ref, ew_ref, xsel_ref, y_out, state_ref,
                    *, d_inner, heads, cps):
    @pl.when(pl.program_id(1) == 0)
    def _():
        state_ref[...] = jnp.zeros_like(state_ref)

    gn = SSD_GROUPS * D_STATE
    lc = SSD_CHUNK
    for k in range(cps - 1, -1, -1):
        rows = slice(k * lc, (k + 1) * lc)
        xbc = xbc_ref[0, rows, :]
        y_out[0, rows, :] = _ssd_chunk(
            xbc[:, 0:d_inner], xbc[:, d_inner:d_inner + gn], xbc[:, d_inner + gn:],
            acum_ref[0, rows, :], tr_ref[0, :, rows], ew_ref[0, rows, V7X_LANES:2 * V7X_LANES],
            xsel_ref[...], state_ref, True, heads, heads)


def _ssd_fwd_kernel(xbc_ref, acum_ref, tr_ref, ew_ref, xsel_ref, yb_ref, z_ref, x_ref,
                    dexp_ref, gg_ref, wout_ref, o_ref, state_ref, yn_scr, *, d_inner, heads, cps):
    @pl.when(pl.program_id(1) == 0)
    def _():
        state_ref[...] = jnp.zeros_like(state_ref)

    gn = SSD_GROUPS * D_STATE
    gw = d_inner // SSD_GROUPS
    lc = SSD_CHUNK
    for k in range(cps):
        rows = slice(k * lc, (k + 1) * lc)
        xbc = xbc_ref[0, rows, :]
        xs = xbc[:, 0:d_inner]
        y = _ssd_chunk(xs, xbc[:, d_inner:d_inner + gn], xbc[:, d_inner + gn:],
                       acum_ref[0, rows, :], tr_ref[0, :, rows], ew_ref[0, rows, 0:V7X_LANES],
                       xsel_ref[...], state_ref, False, 0, heads)
        y = y + yb_ref[0, rows, :] + xs.astype(F32) * dexp_ref[...]
        y = y * z_ref[0, rows, :].astype(F32)
        for g in range(SSD_GROUPS):
            yg = y[:, g * gw:(g + 1) * gw]
            yg = yg * lax.rsqrt(jnp.mean(yg * yg, axis=-1, keepdims=True) + EPS)
            yn_scr[rows, g * gw:(g + 1) * gw] = (yg * gg_ref[:, g * gw:(g + 1) * gw]).astype(BF16)
    o_ref[0] = x_ref[0] + _dot(yn_scr[...], wout_ref[...])


def _ssd_mixer(x, norm_g, w_in, conv_w, conv_b, dt_bias, a_log, d_skip, gate_g, w_out,
               *, tm=512, fc=512, cps=4):
    bsz, s_len, d = x.shape
    d_inner = w_out.shape[0]
    heads = d_skip.shape[0]
    conv_dim = conv_w.shape[1]
    tm = min(tm, s_len)
    w_bf = w_in.astype(BF16)
    dt_pad = V7X_LANES - 2 * heads
    w_dt = jnp.pad(w_bf[:, d_inner + conv_dim:], ((0, 0), (0, dt_pad)))
    dt_b = jnp.pad(dt_bias.reshape(1, 2 * heads).astype(F32), ((0, 0), (0, dt_pad)))
    tile = lambda w: pl.BlockSpec((1, tm, w), lambda b, i: (b, i, 0))
    sz, xbc, dt = pl.pallas_call(
        functools.partial(_ssd_in_kernel, tm=tm, fc=fc, d_inner=d_inner, conv_dim=conv_dim),
        grid=(bsz, s_len // tm),
        in_specs=_halo_specs(tm, s_len, d) + [
            _full((1, d)), _full(w_in.shape), _full((d, V7X_LANES)), _full(conv_w.shape),
            _full((1, conv_dim)), _full((1, V7X_LANES))],
        out_specs=[tile(d_inner), tile(conv_dim), tile(V7X_LANES)],
        out_shape=[jax.ShapeDtypeStruct((bsz, s_len, d_inner), BF16),
                   jax.ShapeDtypeStruct((bsz, s_len, conv_dim), BF16),
                   jax.ShapeDtypeStruct((bsz, s_len, V7X_LANES), F32)],
        scratch_shapes=[pltpu.VMEM((tm + 2 * F32_SUBLANES, d), BF16),
                        pltpu.VMEM((d // V7X_LANES, tm, V7X_LANES), F32)],
        compiler_params=pltpu.CompilerParams(
            dimension_semantics=("parallel", "parallel"), vmem_limit_bytes=V7X_VMEM_LIMIT),
        name="ssd_in",
    )(x, x, x, norm_g.reshape(1, d), w_bf, w_dt, conv_w, conv_b.reshape(1, conv_dim), dt_b)

    lc = SSD_CHUNK
    cps = min(cps, s_len // lc)
    nc = s_len // (lc * cps)
    a_lanes = jnp.pad((-jnp.exp(a_log.astype(F32))).reshape(1, 2 * heads), ((0, 0), (0, dt_pad)))
    step = lambda w: pl.BlockSpec((1, cps * lc, w), lambda b, c: (b, c, 0))
    acum, tr, ew = pl.pallas_call(
        functools.partial(_ssd_decay_kernel, heads=heads, cps=cps),
        grid=(bsz, nc),
        in_specs=[step(V7X_LANES), _full((1, V7X_LANES)), _full((V7X_LANES, 1))],
        out_specs=[step(V7X_LANES), pl.BlockSpec((1, V7X_LANES, cps * lc), lambda b, c: (b, 0, c)),
                   step(2 * V7X_LANES)],
        out_shape=[jax.ShapeDtypeStruct((bsz, s_len, V7X_LANES), F32),
                   jax.ShapeDtypeStruct((bsz, V7X_LANES, s_len), F32),
                   jax.ShapeDtypeStruct((bsz, s_len, 2 * V7X_LANES), BF16)],
        compiler_params=pltpu.CompilerParams(
            dimension_semantics=("parallel", "parallel"), vmem_limit_bytes=V7X_VMEM_LIMIT),
        name="ssd_decay",
    )(dt, a_lanes, a_lanes.reshape(V7X_LANES, 1))

    xr = lax.broadcasted_iota(jnp.int32, (4 * heads, 2 * d_inner), 0)
    xc = lax.broadcasted_iota(jnp.int32, (4 * heads, 2 * d_inner), 1)
    xsel = ((xr % heads == (xc % d_inner) // SSD_HEAD_DIM)
            & (xr // (2 * heads) == xc // d_inner)).astype(BF16)
    state = pltpu.VMEM((SSD_GROUPS, D_STATE, d_inner // SSD_GROUPS), F32)
    rev = lambda w: pl.BlockSpec((1, cps * lc, w), lambda b, c: (b, nc - 1 - c, 0))
    fwd = step
    y_b = pl.pallas_call(
        functools.partial(_ssd_bwd_kernel, d_inner=d_inner, heads=heads, cps=cps),
        grid=(bsz, nc),
        in_specs=[rev(conv_dim), rev(V7X_LANES),
                  pl.BlockSpec((1, V7X_LANES, cps * lc), lambda b, c: (b, 0, nc - 1 - c)),
                  rev(2 * V7X_LANES), _full(xsel.shape)],
        out_specs=rev(d_inner),
        out_shape=jax.ShapeDtypeStruct((bsz, s_len, d_inner), F32),
        scratch_shapes=[state],
        compiler_params=pltpu.CompilerParams(
            dimension_semantics=("parallel", "arbitrary"), vmem_limit_bytes=V7X_VMEM_LIMIT),
        name="ssd_scan_bwd",
    )(xbc, acum, tr, ew, xsel)

    return pl.pallas_call(
        functools.partial(_ssd_fwd_kernel, d_inner=d_inner, heads=heads, cps=cps),
        grid=(bsz, nc),
        in_specs=[fwd(conv_dim), fwd(V7X_LANES),
                  pl.BlockSpec((1, V7X_LANES, cps * lc), lambda b, c: (b, 0, c)),
                  fwd(2 * V7X_LANES), _full(xsel.shape), fwd(d_inner), fwd(d_inner), fwd(d),
                  _full((1, d_inner)), _full((1, d_inner)), _full(w_out.shape)],
        out_specs=fwd(d),
        out_shape=jax.ShapeDtypeStruct(x.shape, F32),
        scratch_shapes=[state, pltpu.VMEM((cps * lc, d_inner), BF16)],
        compiler_params=pltpu.CompilerParams(
            dimension_semantics=("parallel", "arbitrary"), vmem_limit_bytes=V7X_VMEM_LIMIT),
        name="ssd_scan_fwd",
    )(xbc, acum, tr, ew, xsel, y_b, sz, x,
      jnp.broadcast_to(d_skip.astype(F32)[:, None], (heads, SSD_HEAD_DIM)).reshape(1, d_inner),
      gate_g.reshape(1, d_inner).astype(F32), w_out.astype(BF16))


def kernel(x, attn_norm, attn_w_qkv, attn_q_norm, attn_k_norm, attn_sink, attn_w_o, ssd_norm, ssd_w_in, ssd_conv_w, ssd_conv_b, ssd_dt_bias, ssd_a_log, ssd_d, ssd_gate_norm, ssd_w_out, ffn_norm, ffn_w_up, ffn_conv_w, ffn_conv_b, ffn_w_down):
    depth = ffn_norm.shape[0]
    rope = _rope_tables(x.shape[1])
    for i in range(depth):
        j = i // 2
        if i % 2 == 0:
            x = _window_attention(x, attn_norm[j], attn_w_qkv[j], attn_q_norm[j], attn_k_norm[j],
                                  attn_sink[j], attn_w_o[j], rope)
        else:
            x = _ssd_mixer(x, ssd_norm[j], ssd_w_in[j], ssd_conv_w[j], ssd_conv_b[j],
                           ssd_dt_bias[j], ssd_a_log[j], ssd_d[j], ssd_gate_norm[j], ssd_w_out[j])
        x = _conv_ffn(x, ffn_norm[i], ffn_w_up[i], ffn_conv_w[i], ffn_conv_b[i], ffn_w_down[i])
    return x
```

```python
import functools

import jax
import jax.numpy as jnp
from jax import lax
from jax.experimental import pallas as pl
from jax.experimental.pallas import tpu as pltpu

F32 = jnp.float32
BF16 = jnp.bfloat16

EPS = 1e-6
HEAD_DIM = 64
ROT_DIM = HEAD_DIM // 4
ROPE_THETA = 500000.0
ATTN_BLOCK = 128
SSD_HEAD_DIM = 64
SSD_GROUPS = 8
D_STATE = 128
SSD_CHUNK = 128

V7X_LANES = 128
F32_SUBLANES = 8
V7X_VMEM_LIMIT = 56 * 1024 * 1024


def _rms(x, g):
    return x * lax.rsqrt(jnp.mean(x * x, axis=-1, keepdims=True) + EPS) * g


def _silu(x):
    h = 0.5 * x
    return h + h * jnp.tanh(h)


def _dot(a, b):
    return jnp.dot(a, b, preferred_element_type=F32)


def _dot_nt(a, b):
    return lax.dot_general(a, b, (((1,), (1,)), ((), ())), preferred_element_type=F32)


def _dot_tn(a, b):
    return lax.dot_general(a, b, (((0,), (0,)), ((), ())), preferred_element_type=F32)


def _halo_specs(tm, s_len, d):
    r = tm // F32_SUBLANES
    last = s_len // F32_SUBLANES - 1
    return [
        pl.BlockSpec((1, F32_SUBLANES, d), lambda b, i: (b, jnp.maximum(i * r - 1, 0), 0)),
        pl.BlockSpec((1, tm, d), lambda b, i: (b, i, 0)),
        pl.BlockSpec((1, F32_SUBLANES, d), lambda b, i: (b, jnp.minimum((i + 1) * r, last), 0)),
    ]


def _full(shape):
    return pl.BlockSpec(shape, lambda b, i: (0,) * len(shape), pipeline_mode=pl.Buffered(1))


def _layer(stacked, j):
    return pl.BlockSpec((1,) + stacked.shape[1:], lambda b, i: (j, 0, 0), pipeline_mode=pl.Buffered(1))


def _perm_base(a, tm):
    p = tm // F32_SUBLANES
    t = a * F32_SUBLANES
    return (t % p) * F32_SUBLANES + t // p


def _fill_normed(hn_scr, perm_scr, xp_ref, x_ref, xn_ref, g, tm):
    i = pl.program_id(1)
    n = pl.num_programs(1)
    h = F32_SUBLANES
    keep_p = (i > 0).astype(F32)
    keep_n = (i < n - 1).astype(F32)
    halo = jnp.concatenate([_rms(xp_ref[0], g) * keep_p, _rms(xn_ref[0], g) * keep_n], axis=0)
    hn_scr[0:2 * h, :] = halo.astype(BF16)
    xn = _rms(x_ref[0], g)
    slabs = xn.shape[1] // V7X_LANES
    for a in range(tm // h):
        for j in range(slabs):
            perm_scr[j, pl.ds(_perm_base(a, tm), h, stride=h), :] = (
                xn[a * h:(a + 1) * h, j * V7X_LANES:(j + 1) * V7X_LANES])
    for j in range(slabs):
        hn_scr[2 * h:2 * h + tm, j * V7X_LANES:(j + 1) * V7X_LANES] = perm_scr[j].astype(BF16)


def _unpermute(perm_scr, y, tm):
    h = F32_SUBLANES
    slabs = y.shape[1] // V7X_LANES
    for j in range(slabs):
        perm_scr[j] = y[:, j * V7X_LANES:(j + 1) * V7X_LANES]
    rows = []
    for a in range(tm // h):
        rows.append(jnp.concatenate(
            [perm_scr[j, pl.ds(_perm_base(a, tm), h, stride=h), :] for j in range(slabs)], axis=1))
    return jnp.concatenate(rows, axis=0)


def _dwconv(hh, w, b, tm):
    h = F32_SUBLANES
    k_w = w.shape[0]
    pad = k_w // 2
    hp, hx, hm = hh[0:h], hh[h:2 * h], hh[2 * h:]
    sub = lax.broadcasted_iota(jnp.int32, hp.shape, 0)
    before = []
    for e in range(pad, 0, -1):
        src = jnp.where(sub == h - 1, pltpu.roll(hp, e - 1, 0) if e > 1 else hp, hm[tm - e * h:tm - (e - 1) * h])
        before.append(pltpu.roll(src, 1, 0))
    after = []
    for e in range(pad):
        src = jnp.where(sub == 0, pltpu.roll(hx, h - e, 0) if e > 0 else hx, hm[e * h:(e + 1) * h])
        after.append(pltpu.roll(src, h - 1, 0))
    ext = jnp.concatenate(before + [hm] + after, axis=0)
    y = b + ext[0:tm] * w[0:1]
    for k in range(1, k_w):
        y = y + ext[k * h:k * h + tm] * w[k:k + 1]
    return y


def _ffn_kernel(xp_ref, x_ref, xn_ref, g_ref, wup_ref, cw_ref, cb_ref, wdn_ref, o_ref, hn_scr,
                perm_scr, act_scr, *, tm, fc, d_ff):
    _fill_normed(hn_scr, perm_scr, xp_ref, x_ref, xn_ref, g_ref[...], tm)
    hn = hn_scr[...]
    for c in range(d_ff // fc):
        lo, hi = c * fc, (c + 1) * fc
        hg = _dot(hn, wup_ref[0, :, lo:hi])
        hv = _dot(hn, wup_ref[0, :, d_ff + lo:d_ff + hi])
        gate = _dwconv(hg, cw_ref[:, lo:hi], cb_ref[:, lo:hi], tm)
        val = _dwconv(hv, cw_ref[:, d_ff + lo:d_ff + hi], cb_ref[:, d_ff + lo:d_ff + hi], tm)
        act_scr[:, lo:hi] = (_silu(gate) * val).astype(BF16)
    o_ref[0] = x_ref[0] + _unpermute(perm_scr, _dot(act_scr[...], wdn_ref[0]), tm)


def _conv_ffn(x, g, w_up, conv_w, conv_b, w_down, layer, *, tm=1024, fc=256):
    bsz, s_len, d = x.shape
    d_ff = w_down.shape[1]
    tm = min(tm, s_len)
    kern = functools.partial(_ffn_kernel, tm=tm, fc=fc, d_ff=d_ff)
    return pl.pallas_call(
        kern,
        grid=(bsz, s_len // tm),
        in_specs=_halo_specs(tm, s_len, d) + [
            _full((1, d)), _layer(w_up, layer), _full(conv_w.shape), _full((1, 2 * d_ff)),
            _layer(w_down, layer)],
        out_specs=pl.BlockSpec((1, tm, d), lambda b, i: (b, i, 0)),
        out_shape=jax.ShapeDtypeStruct(x.shape, F32),
        scratch_shapes=[pltpu.VMEM((tm + 2 * F32_SUBLANES, d), BF16),
                        pltpu.VMEM((d // V7X_LANES, tm, V7X_LANES), F32),
                        pltpu.VMEM((tm, d_ff), BF16)],
        compiler_params=pltpu.CompilerParams(
            dimension_semantics=("parallel", "parallel"), vmem_limit_bytes=V7X_VMEM_LIMIT),
        name="conv_ffn",
    )(x, x, x, g.reshape(1, d), w_up, conv_w, conv_b.reshape(1, -1), w_down)


def _rope_tables(s_len):
    half = ROT_DIM // 2
    pos = jnp.arange(s_len, dtype=F32)
    inv_freq = ROPE_THETA ** (-(jnp.arange(0, ROT_DIM, 2, dtype=F32) / ROT_DIM))
    ang = pos[:, None] * inv_freq[None, :]
    cos, sin = jnp.cos(ang), jnp.sin(ang)
    rest = HEAD_DIM - ROT_DIM
    c = jnp.concatenate([cos, cos, jnp.ones((s_len, rest), F32)], axis=1)
    s1 = jnp.concatenate([-sin, jnp.zeros((s_len, half + rest), F32)], axis=1)
    s2 = jnp.concatenate([jnp.zeros((s_len, half), F32), sin, jnp.zeros((s_len, rest), F32)], axis=1)
    rep = V7X_LANES // HEAD_DIM
    return jnp.tile(c, (1, rep)), jnp.tile(s1, (1, rep)), jnp.tile(s2, (1, rep))


def _norm_rope(t, red, expd, g, c, s1, s2, scale):
    half = ROT_DIM // 2
    max_heads = 16
    rs = lax.rsqrt(_dot((t * t).astype(BF16), red) + EPS)
    hi = rs.astype(BF16).astype(F32)
    lane = lax.broadcasted_iota(jnp.int32, rs.shape, 1)
    packed = jnp.where(lane < max_heads, hi, pltpu.roll(rs - hi, max_heads, 1))
    tn = t * _dot(packed.astype(BF16), expd) * g
    outs = []
    for j in range(t.shape[1] // V7X_LANES):
        tc = tn[:, j * V7X_LANES:(j + 1) * V7X_LANES]
        tr = tc * c + pltpu.roll(tc, V7X_LANES - half, 1) * s1 + pltpu.roll(tc, half, 1) * s2
        outs.append(tr * scale if scale != 1.0 else tr)
    return jnp.concatenate(outs, axis=1)


def _replicate_heads(t, group):
    lane = lax.broadcasted_iota(jnp.int32, (t.shape[0], V7X_LANES), 1)
    first = lane < HEAD_DIM
    outs = []
    for j in range(t.shape[1] // V7X_LANES):
        tc = t[:, j * V7X_LANES:(j + 1) * V7X_LANES]
        tc_sw = pltpu.roll(tc, HEAD_DIM, 1)
        even = jnp.where(first, tc, tc_sw)
        odd = jnp.where(first, tc_sw, tc)
        outs += [even] * (group // 2) + [odd] * (group // 2)
    return jnp.concatenate(outs, axis=1)


def _qkv_kernel(x_ref, g_ref, w_ref, qg_ref, kg_ref, redq_ref, expq_ref, redk_ref, expk_ref,
                c_ref, s1_ref, s2_ref, q_out, k_out, v_out, *, q_dim, kv_dim, group):
    hn = _rms(x_ref[0], g_ref[...]).astype(BF16)
    c, s1, s2 = c_ref[...], s1_ref[...], s2_ref[...]
    q = _dot(hn, w_ref[0, :, 0:q_dim])
    q_out[0] = _norm_rope(q, redq_ref[...], expq_ref[...], qg_ref[...], c, s1, s2,
                          HEAD_DIM ** -0.5).astype(BF16)
    k = _dot(hn, w_ref[0, :, q_dim:q_dim + kv_dim])
    kr = _norm_rope(k, redk_ref[...], expk_ref[...], kg_ref[...], c, s1, s2, 1.0)
    k_out[0] = _replicate_heads(kr, group).astype(BF16)
    v = _dot(hn, w_ref[0, :, q_dim + kv_dim:q_dim + 2 * kv_dim])
    v_out[0] = _replicate_heads(v, group).astype(BF16)


def _attn_kernel(sink_ref, q_ref, kp_ref, k_ref, kn_ref, vp_ref, v_ref, vn_ref, x_ref, wo_ref,
                 o_ref, o_scr, *, tq, n_kv, group):
    i = pl.program_id(1)
    n = pl.num_programs(1)
    blk = ATTN_BLOCK
    gw = group * HEAD_DIM
    k_all = jnp.concatenate([kp_ref[0], k_ref[0], kn_ref[0]], axis=0)
    v_all = jnp.concatenate([vp_ref[0], v_ref[0], vn_ref[0]], axis=0)
    row = lax.broadcasted_iota(jnp.int32, (group * blk, 3 * blk), 0) % blk
    col = lax.broadcasted_iota(jnp.int32, (group * blk, 3 * blk), 1)
    band = (col >= row) & (col <= row + 2 * blk)
    head_of_lane = lax.broadcasted_iota(jnp.int32, (blk, gw), 1) // HEAD_DIM
    head_of_row = lax.broadcasted_iota(jnp.int32, (group * blk, 1), 0) // blk
    nblk = tq // blk
    for jb in range(nblk):
        valid = band
        if jb == 0:
            valid = valid & ((col >= blk) | (i > 0))
        if jb == nblk - 1:
            valid = valid & ((col < 2 * blk) | (i < n - 1))
        for kh in range(n_kv):
            qg = q_ref[0, jb * blk:(jb + 1) * blk, kh * gw:(kh + 1) * gw]
            kw = k_all[jb * blk:(jb + 3) * blk, kh * gw:(kh + 1) * gw]
            vw = v_all[jb * blk:(jb + 3) * blk, kh * gw:(kh + 1) * gw]
            qs = jnp.concatenate(
                [jnp.where(head_of_lane == hl, qg, jnp.zeros_like(qg)) for hl in range(group)], axis=0)
            sink = jnp.full((group * blk, 1), sink_ref[kh * group + group - 1], F32)
            for hl in range(group - 2, -1, -1):
                sink = jnp.where(head_of_row == hl, sink_ref[kh * group + hl], sink)
            s = jnp.where(valid, _dot_nt(qs, kw), -1e30)
            m = jnp.maximum(jnp.max(s, axis=-1, keepdims=True), sink)
            p = jnp.exp(s - m)
            denom = jnp.sum(p, axis=-1, keepdims=True) + jnp.exp(sink - m)
            o = _dot(p.astype(BF16), vw) / denom
            og = o[(group - 1) * blk:group * blk]
            for hl in range(group - 2, -1, -1):
                og = jnp.where(head_of_lane == hl, o[hl * blk:(hl + 1) * blk], og)
            o_scr[jb * blk:(jb + 1) * blk, kh * gw:(kh + 1) * gw] = og.astype(BF16)
    o_ref[0] = x_ref[0] + _dot(o_scr[...], wo_ref[0])


def _window_attention(x, norm_g, w_qkv, q_g, k_g, sink, w_o, layer, rope, *, tm=1024, tq=512):
    bsz, s_len, d = x.shape
    n_heads = sink.shape[0]
    q_dim = w_o.shape[1]
    kv_dim = (w_qkv.shape[2] - q_dim) // 2
    n_kv = kv_dim // HEAD_DIM
    group = n_heads // n_kv
    tm = min(tm, s_len)
    tq = min(tq, s_len)
    c, s1, s2 = rope

    def reduce_expand(width):
        i = lax.broadcasted_iota(jnp.int32, (width, V7X_LANES), 0)
        j = lax.broadcasted_iota(jnp.int32, (width, V7X_LANES), 1)
        red = jnp.where(i // HEAD_DIM == j, 1.0 / HEAD_DIM, 0.0).astype(BF16)
        expd = ((j.T % 16 == i.T // HEAD_DIM) & (j.T < 32)).astype(BF16)
        return red, expd

    redq, expq = reduce_expand(q_dim)
    redk, expk = reduce_expand(kv_dim)
    tile = lambda w: pl.BlockSpec((1, tm, w), lambda b, i: (b, i, 0))
    rope_spec = pl.BlockSpec((tm, V7X_LANES), lambda b, i: (i, 0))
    q, k, v = pl.pallas_call(
        functools.partial(_qkv_kernel, q_dim=q_dim, kv_dim=kv_dim, group=group),
        grid=(bsz, s_len // tm),
        in_specs=[tile(d), _full((1, d)), _layer(w_qkv, layer), _full((1, q_dim)),
                  _full((1, kv_dim)), _full(redq.shape), _full(expq.shape), _full(redk.shape),
                  _full(expk.shape), rope_spec, rope_spec, rope_spec],
        out_specs=[tile(q_dim), tile(q_dim), tile(q_dim)],
        out_shape=[jax.ShapeDtypeStruct((bsz, s_len, q_dim), BF16)] * 3,
        compiler_params=pltpu.CompilerParams(
            dimension_semantics=("parallel", "parallel"), vmem_limit_bytes=V7X_VMEM_LIMIT),
        name="attn_qkv",
    )(x, norm_g.reshape(1, d), w_qkv, jnp.tile(q_g, n_heads).reshape(1, q_dim),
      jnp.tile(k_g, n_kv).reshape(1, kv_dim), redq, expq, redk, expk, c, s1, s2)

    r = tq // ATTN_BLOCK
    last = s_len // ATTN_BLOCK - 1
    prev_spec = pl.BlockSpec((1, ATTN_BLOCK, q_dim), lambda b, i: (b, jnp.maximum(i * r - 1, 0), 0))
    main_spec = pl.BlockSpec((1, tq, q_dim), lambda b, i: (b, i, 0))
    next_spec = pl.BlockSpec((1, ATTN_BLOCK, q_dim), lambda b, i: (b, jnp.minimum((i + 1) * r, last), 0))
    return pl.pallas_call(
        functools.partial(_attn_kernel, tq=tq, n_kv=n_kv, group=group),
        grid=(bsz, s_len // tq),
        in_specs=[pl.BlockSpec(memory_space=pltpu.SMEM), main_spec,
                  prev_spec, main_spec, next_spec, prev_spec, main_spec, next_spec,
                  pl.BlockSpec((1, tq, d), lambda b, i: (b, i, 0)), _layer(w_o, layer)],
        out_specs=pl.BlockSpec((1, tq, d), lambda b, i: (b, i, 0)),
        out_shape=jax.ShapeDtypeStruct(x.shape, F32),
        scratch_shapes=[pltpu.VMEM((tq, q_dim), BF16)],
        compiler_params=pltpu.CompilerParams(
            dimension_semantics=("parallel", "parallel"), vmem_limit_bytes=V7X_VMEM_LIMIT),
        name="attn_core",
    )(sink.astype(F32), q, k, k, k, v, v, v, x, w_o)


def _softplus(x):
    return jnp.maximum(x, 0.0) + jnp.log1p(jnp.exp(-jnp.abs(x)))


def _ssd_in_kernel(xp_ref, x_ref, xn_ref, g_ref, w_ref, wdt_ref, cw_ref, cb_ref, dtb_ref,
                   z_out, xbc_out, dt_out, hn_scr, perm_scr, *, tm, fc, d_inner, conv_dim):
    g = g_ref[...]
    _fill_normed(hn_scr, perm_scr, xp_ref, x_ref, xn_ref, g, tm)
    hn_nat = _rms(x_ref[0], g).astype(BF16)
    z_out[0] = _silu(_dot(hn_nat, w_ref[0, :, 0:d_inner])).astype(BF16)
    dt_out[0] = _softplus(_dot(hn_nat, wdt_ref[...]) + dtb_ref[...])
    hn = hn_scr[...]
    for c in range(conv_dim // fc):
        lo, hi = c * fc, (c + 1) * fc
        h = _dot(hn, w_ref[0, :, d_inner + lo:d_inner + hi])
        y = _unpermute(perm_scr, _dwconv(h, cw_ref[:, lo:hi], cb_ref[:, lo:hi], tm), tm)
        xbc_out[0, :, lo:hi] = _silu(y).astype(BF16)


def _pieces(x, n):
    out = []
    for _ in range(n - 1):
        p = x.astype(BF16).astype(F32)
        out.append(p)
        x = x - p
    out.append(x.astype(BF16).astype(F32))
    return out


def _ssd_decay_kernel(dt_ref, a_ref, at_ref, acum_out, tr_out, ew_out, *, heads, cps):
    lc = SSD_CHUNK
    row = lax.broadcasted_iota(jnp.int32, (lc, lc), 0)
    col = lax.broadcasted_iota(jnp.int32, (lc, lc), 1)
    lower = (row >= col).astype(BF16)
    upper = (row <= col).astype(BF16)
    fwd_lane = lax.broadcasted_iota(jnp.int32, (lc, V7X_LANES), 1) < heads
    fwd_row = lax.broadcasted_iota(jnp.int32, (V7X_LANES, lc), 0) < heads
    for k in range(cps):
        rows = slice(k * lc, (k + 1) * lc)
        dt = dt_ref[0, rows, :]
        dtt = dt.T
        ps = [p.astype(BF16) for p in _pieces(dt * a_ref[...], 3)]
        cum_f = (_dot(lower, ps[0]) + _dot(lower, ps[1])) + _dot(lower, ps[2])
        cum_b = (_dot(upper, ps[0]) + _dot(upper, ps[1])) + _dot(upper, ps[2])
        a_cum = jnp.where(fwd_lane, cum_f, cum_b)
        qs = [q.astype(BF16) for q in _pieces(dtt * at_ref[...], 3)]
        cum_tf = (_dot(qs[0], upper) + _dot(qs[1], upper)) + _dot(qs[2], upper)
        cum_tb = (_dot(qs[0], lower) + _dot(qs[1], lower)) + _dot(qs[2], lower)
        a_cum_t = jnp.where(fwd_row, cum_tf, cum_tb)
        a_end = jnp.where(fwd_lane[0:1], a_cum[lc - 1:lc, :], a_cum[0:1, :])
        e1, e2 = _pieces(jnp.exp(a_cum), 2)
        w1, w2 = _pieces(dt * jnp.exp(a_end - a_cum), 2)
        acum_out[0, rows, :] = a_cum
        half = V7X_LANES // 2
        tr_out[0, :, rows] = jnp.concatenate([a_cum_t[0:half], dtt[0:half]], axis=0)
        for d in range(2):
            lanes = slice(d * heads, (d + 1) * heads)
            ew_out[0, rows, d * V7X_LANES:(d + 1) * V7X_LANES] = jnp.concatenate(
                [e1[:, lanes], e2[:, lanes], w1[:, lanes], w2[:, lanes]], axis=1).astype(BF16)


def _ssd_chunk(xs, bm, cm, acum, tr, ewl, xsel, state_ref, reverse, off, heads):
    n_chunk, d_inner = xs.shape
    hpg = heads // SSD_GROUPS
    gw = hpg * SSD_HEAD_DIM
    dtt_off = V7X_LANES // 2 + off
    row = lax.broadcasted_iota(jnp.int32, (n_chunk, n_chunk), 0)
    col = lax.broadcasted_iota(jnp.int32, (n_chunk, n_chunk), 1)
    later = (row <= col) if reverse else (row >= col)
    end = 0 if reverse else n_chunk - 1
    ew = _dot(ewl, xsel)
    e_exp, w_exp = ew[:, 0:d_inner], ew[:, d_inner:2 * d_inner]
    lane_head = lax.broadcasted_iota(jnp.int32, (n_chunk, gw), 1) // SSD_HEAD_DIM
    ys = []
    for g in range(SSD_GROUPS):
        b_g = bm[:, g * D_STATE:(g + 1) * D_STATE]
        c_g = cm[:, g * D_STATE:(g + 1) * D_STATE]
        x_g = xs[:, g * gw:(g + 1) * gw]
        cb = _dot_nt(c_g, b_g)
        ms, xb = [], []
        for r in range(hpg):
            h = g * hpg + r
            diff = acum[:, off + h:off + h + 1] - tr[off + h:off + h + 1, :]
            decay = jnp.exp(jnp.where(later, diff, -1e30))
            ms.append((cb * decay * tr[dtt_off + h:dtt_off + h + 1, :]).astype(BF16))
            xb.append(jnp.where(lane_head == r, x_g, jnp.zeros_like(x_g)))
        y_diag = _dot(jnp.concatenate(ms, axis=1), jnp.concatenate(xb, axis=0))
        st = state_ref[g]
        ys.append(y_diag + _dot(c_g, st.astype(BF16)) * e_exp[:, g * gw:(g + 1) * gw])
        xw = (x_g.astype(F32) * w_exp[:, g * gw:(g + 1) * gw]).astype(BF16)
        state_ref[g] = e_exp[end:end + 1, g * gw:(g + 1) * gw] * st + _dot_tn(b_g, xw)
    return jnp.concatenate(ys, axis=1)


def _ssd_bwd_kernel(xbc_ref, acum_ref, tr_ref, ew_ref, xsel_ref, y_out, state_ref,
                    *, d_inner, heads, cps):
    @pl.when(pl.program_id(1) == 0)
    def _():
        state_ref[...] = jnp.zeros_like(state_ref)

    gn = SSD_GROUPS * D_STATE
    lc = SSD_CHUNK
    for k in range(cps - 1, -1, -1):
        rows = slice(k * lc, (k + 1) * lc)
        xbc = xbc_ref[0, rows, :]
        y_out[0, rows, :] = _ssd_chunk(
            xbc[:, 0:d_inner], xbc[:, d_inner:d_inner + gn], xbc[:, d_inner + gn:],
            acum_ref[0, rows, :], tr_ref[0, :, rows], ew_ref[0, rows, V7X_LANES:2 * V7X_LANES],
            xsel_ref[...], state_ref, True, heads, heads)


def _ssd_fwd_kernel(xbc_ref, acum_ref, tr_ref, ew_ref, xsel_ref, yb_ref, z_ref, x_ref,
                    dexp_ref, gg_ref, wout_ref, o_ref, state_ref, yn_scr, *, d_inner, heads, cps):
    @pl.when(pl.program_id(1) == 0)
    def _():
        state_ref[...] = jnp.zeros_like(state_ref)

    gn = SSD_GROUPS * D_STATE
    gw = d_inner // SSD_GROUPS
    lc = SSD_CHUNK
    for k in range(cps):
        rows = slice(k * lc, (k + 1) * lc)
        xbc = xbc_ref[0, rows, :]
        xs = xbc[:, 0:d_inner]
        y = _ssd_chunk(xs, xbc[:, d_inner:d_inner + gn], xbc[:, d_inner + gn:],
                       acum_ref[0, rows, :], tr_ref[0, :, rows], ew_ref[0, rows, 0:V7X_LANES],
                       xsel_ref[...], state_ref, False, 0, heads)
        y = y + yb_ref[0, rows, :] + xs.astype(F32) * dexp_ref[...]
        y = y * z_ref[0, rows, :].astype(F32)
        for g in range(SSD_GROUPS):
            yg = y[:, g * gw:(g + 1) * gw]
            yg = yg * lax.rsqrt(jnp.mean(yg * yg, axis=-1, keepdims=True) + EPS)
            yn_scr[rows, g * gw:(g + 1) * gw] = (yg * gg_ref[:, g * gw:(g + 1) * gw]).astype(BF16)
    o_ref[0] = x_ref[0] + _dot(yn_scr[...], wout_ref[0])


def _ssd_mixer(x, norm_g, w_in, conv_w, conv_b, dt_bias, a_log, d_skip, gate_g, w_out, layer,
               *, tm=512, fc=512, cps=4):
    bsz, s_len, d = x.shape
    d_inner = w_out.shape[1]
    heads = d_skip.shape[0]
    conv_dim = conv_w.shape[1]
    tm = min(tm, s_len)
    dt_pad = V7X_LANES - 2 * heads
    w_dt = jnp.pad(w_in[layer, :, d_inner + conv_dim:], ((0, 0), (0, dt_pad)))
    dt_b = jnp.pad(dt_bias.reshape(1, 2 * heads).astype(F32), ((0, 0), (0, dt_pad)))
    tile = lambda w: pl.BlockSpec((1, tm, w), lambda b, i: (b, i, 0))
    sz, xbc, dt = pl.pallas_call(
        functools.partial(_ssd_in_kernel, tm=tm, fc=fc, d_inner=d_inner, conv_dim=conv_dim),
        grid=(bsz, s_len // tm),
        in_specs=_halo_specs(tm, s_len, d) + [
            _full((1, d)), _layer(w_in, layer), _full((d, V7X_LANES)), _full(conv_w.shape),
            _full((1, conv_dim)), _full((1, V7X_LANES))],
        out_specs=[tile(d_inner), tile(conv_dim), tile(V7X_LANES)],
        out_shape=[jax.ShapeDtypeStruct((bsz, s_len, d_inner), BF16),
                   jax.ShapeDtypeStruct((bsz, s_len, conv_dim), BF16),
                   jax.ShapeDtypeStruct((bsz, s_len, V7X_LANES), F32)],
        scratch_shapes=[pltpu.VMEM((tm + 2 * F32_SUBLANES, d), BF16),
                        pltpu.VMEM((d // V7X_LANES, tm, V7X_LANES), F32)],
        compiler_params=pltpu.CompilerParams(
            dimension_semantics=("parallel", "parallel"), vmem_limit_bytes=V7X_VMEM_LIMIT),
        name="ssd_in",
    )(x, x, x, norm_g.reshape(1, d), w_in, w_dt, conv_w, conv_b.reshape(1, conv_dim), dt_b)

    lc = SSD_CHUNK
    cps = min(cps, s_len // lc)
    nc = s_len // (lc * cps)
    a_lanes = jnp.pad((-jnp.exp(a_log.astype(F32))).reshape(1, 2 * heads), ((0, 0), (0, dt_pad)))
    step = lambda w: pl.BlockSpec((1, cps * lc, w), lambda b, c: (b, c, 0))
    acum, tr, ew = pl.pallas_call(
        functools.partial(_ssd_decay_kernel, heads=heads, cps=cps),
        grid=(bsz, nc),
        in_specs=[step(V7X_LANES), _full((1, V7X_LANES)), _full((V7X_LANES, 1))],
        out_specs=[step(V7X_LANES), pl.BlockSpec((1, V7X_LANES, cps * lc), lambda b, c: (b, 0, c)),
                   step(2 * V7X_LANES)],
        out_shape=[jax.ShapeDtypeStruct((bsz, s_len, V7X_LANES), F32),
                   jax.ShapeDtypeStruct((bsz, V7X_LANES, s_len), F32),
                   jax.ShapeDtypeStruct((bsz, s_len, 2 * V7X_LANES), BF16)],
        compiler_params=pltpu.CompilerParams(
            dimension_semantics=("parallel", "parallel"), vmem_limit_bytes=V7X_VMEM_LIMIT),
        name="ssd_decay",
    )(dt, a_lanes, a_lanes.reshape(V7X_LANES, 1))

    xr = lax.broadcasted_iota(jnp.int32, (4 * heads, 2 * d_inner), 0)
    xc = lax.broadcasted_iota(jnp.int32, (4 * heads, 2 * d_inner), 1)
    xsel = ((xr % heads == (xc % d_inner) // SSD_HEAD_DIM)
            & (xr // (2 * heads) == xc // d_inner)).astype(BF16)
    state = pltpu.VMEM((SSD_GROUPS, D_STATE, d_inner // SSD_GROUPS), F32)
    rev = lambda w: pl.BlockSpec((1, cps * lc, w), lambda b, c: (b, nc - 1 - c, 0))
    fwd = step
    y_b = pl.pallas_call(
        functools.partial(_ssd_bwd_kernel, d_inner=d_inner, heads=heads, cps=cps),
        grid=(bsz, nc),
        in_specs=[rev(conv_dim), rev(V7X_LANES),
                  pl.BlockSpec((1, V7X_LANES, cps * lc), lambda b, c: (b, 0, nc - 1 - c)),
                  rev(2 * V7X_LANES), _full(xsel.shape)],
        out_specs=rev(d_inner),
        out_shape=jax.ShapeDtypeStruct((bsz, s_len, d_inner), F32),
        scratch_shapes=[state],
        compiler_params=pltpu.CompilerParams(
            dimension_semantics=("parallel", "arbitrary"), vmem_limit_bytes=V7X_VMEM_LIMIT),
        name="ssd_scan_bwd",
    )(xbc, acum, tr, ew, xsel)

    return pl.pallas_call(
        functools.partial(_ssd_fwd_kernel, d_inner=d_inner, heads=heads, cps=cps),
        grid=(bsz, nc),
        in_specs=[fwd(conv_dim), fwd(V7X_LANES),
                  pl.BlockSpec((1, V7X_LANES, cps * lc), lambda b, c: (b, 0, c)),
                  fwd(2 * V7X_LANES), _full(xsel.shape), fwd(d_inner), fwd(d_inner), fwd(d),
                  _full((1, d_inner)), _full((1, d_inner)), _layer(w_out, layer)],
        out_specs=fwd(d),
        out_shape=jax.ShapeDtypeStruct(x.shape, F32),
        scratch_shapes=[state, pltpu.VMEM((cps * lc, d_inner), BF16)],
        compiler_params=pltpu.CompilerParams(
            dimension_semantics=("parallel", "arbitrary"), vmem_limit_bytes=V7X_VMEM_LIMIT),
        name="ssd_scan_fwd",
    )(xbc, acum, tr, ew, xsel, y_b, sz, x,
      jnp.broadcast_to(d_skip.astype(F32)[:, None], (heads, SSD_HEAD_DIM)).reshape(1, d_inner),
      gate_g.reshape(1, d_inner).astype(F32), w_out)


def kernel(x, attn_norm, attn_w_qkv, attn_q_norm, attn_k_norm, attn_sink, attn_w_o, ssd_norm, ssd_w_in, ssd_conv_w, ssd_conv_b, ssd_dt_bias, ssd_a_log, ssd_d, ssd_gate_norm, ssd_w_out, ffn_norm, ffn_w_up, ffn_conv_w, ffn_conv_b, ffn_w_down):
    depth = ffn_norm.shape[0]
    rope = _rope_tables(x.shape[1])
    w_qkv, w_o = attn_w_qkv.astype(BF16), attn_w_o.astype(BF16)
    w_in, w_out = ssd_w_in.astype(BF16), ssd_w_out.astype(BF16)
    w_up, w_down = ffn_w_up.astype(BF16), ffn_w_down.astype(BF16)
    for i in range(depth):
        j = i // 2
        if i % 2 == 0:
            x = _window_attention(x, attn_norm[j], w_qkv, attn_q_norm[j], attn_k_norm[j],
                                  attn_sink[j], w_o, j, rope)
        else:
            x = _ssd_mixer(x, ssd_norm[j], w_in, ssd_conv_w[j], ssd_conv_b[j], ssd_dt_bias[j],
                           ssd_a_log[j], ssd_d[j], ssd_gate_norm[j], w_out, j)
        x = _conv_ffn(x, ffn_norm[i], w_up, ffn_conv_w[i], ffn_conv_b[i], w_down, i)
    return x
```

```python
import functools

import jax
import jax.numpy as jnp
from jax import lax
from jax.experimental import pallas as pl
from jax.experimental.pallas import tpu as pltpu

F32 = jnp.float32
BF16 = jnp.bfloat16

EPS = 1e-6
HEAD_DIM = 64
ROT_DIM = HEAD_DIM // 4
ROPE_THETA = 500000.0
ATTN_BLOCK = 128
SSD_HEAD_DIM = 64
SSD_GROUPS = 8
D_STATE = 128
SSD_CHUNK = 128

V7X_LANES = 128
F32_SUBLANES = 8
V7X_VMEM_LIMIT = 56 * 1024 * 1024


def _rms(x, g):
    return x * lax.rsqrt(jnp.mean(x * x, axis=-1, keepdims=True) + EPS) * g


def _silu(x):
    h = 0.5 * x
    return h + h * jnp.tanh(h)


def _dot(a, b):
    return jnp.dot(a, b, preferred_element_type=F32)


def _dot_nt(a, b):
    return lax.dot_general(a, b, (((1,), (1,)), ((), ())), preferred_element_type=F32)


def _dot_tn(a, b):
    return lax.dot_general(a, b, (((0,), (0,)), ((), ())), preferred_element_type=F32)


def _halo_specs(tm, s_len, d):
    r = tm // F32_SUBLANES
    last = s_len // F32_SUBLANES - 1
    return [
        pl.BlockSpec((1, F32_SUBLANES, d), lambda b, i: (b, jnp.maximum(i * r - 1, 0), 0)),
        pl.BlockSpec((1, tm, d), lambda b, i: (b, i, 0)),
        pl.BlockSpec((1, F32_SUBLANES, d), lambda b, i: (b, jnp.minimum((i + 1) * r, last), 0)),
    ]


def _full(shape):
    return pl.BlockSpec(shape, lambda b, i: (0,) * len(shape), pipeline_mode=pl.Buffered(1))


def _layer(stacked, j):
    return pl.BlockSpec((1,) + stacked.shape[1:], lambda b, i: (j, 0, 0), pipeline_mode=pl.Buffered(1))


def _perm_base(a, tm):
    p = tm // F32_SUBLANES
    t = a * F32_SUBLANES
    return (t % p) * F32_SUBLANES + t // p


def _fill_normed(hn_scr, perm_scr, xp_ref, x_ref, xn_ref, g, tm):
    i = pl.program_id(1)
    n = pl.num_programs(1)
    h = F32_SUBLANES
    keep_p = (i > 0).astype(F32)
    keep_n = (i < n - 1).astype(F32)
    halo = jnp.concatenate([_rms(xp_ref[0], g) * keep_p, _rms(xn_ref[0], g) * keep_n], axis=0)
    hn_scr[0:2 * h, :] = halo.astype(BF16)
    xn = _rms(x_ref[0], g)
    slabs = xn.shape[1] // V7X_LANES
    for a in range(tm // h):
        for j in range(slabs):
            perm_scr[j, pl.ds(_perm_base(a, tm), h, stride=h), :] = (
                xn[a * h:(a + 1) * h, j * V7X_LANES:(j + 1) * V7X_LANES])
    for j in range(slabs):
        hn_scr[2 * h:2 * h + tm, j * V7X_LANES:(j + 1) * V7X_LANES] = perm_scr[j].astype(BF16)


def _unpermute(perm_scr, y, tm):
    h = F32_SUBLANES
    slabs = y.shape[1] // V7X_LANES
    for j in range(slabs):
        perm_scr[j] = y[:, j * V7X_LANES:(j + 1) * V7X_LANES]
    rows = []
    for a in range(tm // h):
        rows.append(jnp.concatenate(
            [perm_scr[j, pl.ds(_perm_base(a, tm), h, stride=h), :] for j in range(slabs)], axis=1))
    return jnp.concatenate(rows, axis=0)


def _dwconv(hh, w, b, tm):
    h = F32_SUBLANES
    k_w = w.shape[0]
    pad = k_w // 2
    hp, hx, hm = hh[0:h], hh[h:2 * h], hh[2 * h:]
    sub = lax.broadcasted_iota(jnp.int32, hp.shape, 0)
    before = []
    for e in range(pad, 0, -1):
        src = jnp.where(sub == h - 1, pltpu.roll(hp, e - 1, 0) if e > 1 else hp, hm[tm - e * h:tm - (e - 1) * h])
        before.append(pltpu.roll(src, 1, 0))
    after = []
    for e in range(pad):
        src = jnp.where(sub == 0, pltpu.roll(hx, h - e, 0) if e > 0 else hx, hm[e * h:(e + 1) * h])
        after.append(pltpu.roll(src, h - 1, 0))
    ext = jnp.concatenate(before + [hm] + after, axis=0)
    y = b + ext[0:tm] * w[0:1]
    for k in range(1, k_w):
        y = y + ext[k * h:k * h + tm] * w[k:k + 1]
    return y


def _ffn_kernel(xp_ref, x_ref, xn_ref, g_ref, wup_ref, cw_ref, cb_ref, wdn_ref, o_ref, hn_scr,
                perm_scr, act_scr, *, tm, fc, d_ff):
    _fill_normed(hn_scr, perm_scr, xp_ref, x_ref, xn_ref, g_ref[...], tm)
    hn = hn_scr[...]
    for c in range(d_ff // fc):
        lo, hi = c * fc, (c + 1) * fc
        hg = _dot(hn, wup_ref[0, :, lo:hi])
        hv = _dot(hn, wup_ref[0, :, d_ff + lo:d_ff + hi])
        gate = _dwconv(hg, cw_ref[:, lo:hi], cb_ref[:, lo:hi], tm)
        val = _dwconv(hv, cw_ref[:, d_ff + lo:d_ff + hi], cb_ref[:, d_ff + lo:d_ff + hi], tm)
        act_scr[:, lo:hi] = (_silu(gate) * val).astype(BF16)
    o_ref[0] = x_ref[0] + _unpermute(perm_scr, _dot(act_scr[...], wdn_ref[0]), tm)


def _conv_ffn(x, g, w_up, conv_w, conv_b, w_down, layer, *, tm=1024, fc=256):
    bsz, s_len, d = x.shape
    d_ff = w_down.shape[1]
    tm = min(tm, s_len)
    kern = functools.partial(_ffn_kernel, tm=tm, fc=fc, d_ff=d_ff)
    return pl.pallas_call(
        kern,
        grid=(bsz, s_len // tm),
        in_specs=_halo_specs(tm, s_len, d) + [
            _full((1, d)), _layer(w_up, layer), _full(conv_w.shape), _full((1, 2 * d_ff)),
            _layer(w_down, layer)],
        out_specs=pl.BlockSpec((1, tm, d), lambda b, i: (b, i, 0)),
        out_shape=jax.ShapeDtypeStruct(x.shape, F32),
        scratch_shapes=[pltpu.VMEM((tm + 2 * F32_SUBLANES, d), BF16),
                        pltpu.VMEM((d // V7X_LANES, tm, V7X_LANES), F32),
                        pltpu.VMEM((tm, d_ff), BF16)],
        compiler_params=pltpu.CompilerParams(
            dimension_semantics=("parallel", "parallel"), vmem_limit_bytes=V7X_VMEM_LIMIT),
        name="conv_ffn",
    )(x, x, x, g.reshape(1, d), w_up, conv_w, conv_b.reshape(1, -1), w_down)


def _rope_tables(s_len):
    half = ROT_DIM // 2
    pos = jnp.arange(s_len, dtype=F32)
    inv_freq = ROPE_THETA ** (-(jnp.arange(0, ROT_DIM, 2, dtype=F32) / ROT_DIM))
    ang = pos[:, None] * inv_freq[None, :]
    cos, sin = jnp.cos(ang), jnp.sin(ang)
    rest = HEAD_DIM - ROT_DIM
    c = jnp.concatenate([cos, cos, jnp.ones((s_len, rest), F32)], axis=1)
    s1 = jnp.concatenate([-sin, jnp.zeros((s_len, half + rest), F32)], axis=1)
    s2 = jnp.concatenate([jnp.zeros((s_len, half), F32), sin, jnp.zeros((s_len, rest), F32)], axis=1)
    rep = V7X_LANES // HEAD_DIM
    return jnp.tile(c, (1, rep)), jnp.tile(s1, (1, rep)), jnp.tile(s2, (1, rep))


def _norm_rope(t, red, expd, g, c, s1, s2, scale):
    half = ROT_DIM // 2
    max_heads = 16
    rs = lax.rsqrt(_dot((t * t).astype(BF16), red) + EPS)
    hi = rs.astype(BF16).astype(F32)
    lane = lax.broadcasted_iota(jnp.int32, rs.shape, 1)
    packed = jnp.where(lane < max_heads, hi, pltpu.roll(rs - hi, max_heads, 1))
    tn = t * _dot(packed.astype(BF16), expd) * g
    outs = []
    for j in range(t.shape[1] // V7X_LANES):
        tc = tn[:, j * V7X_LANES:(j + 1) * V7X_LANES]
        tr = tc * c + pltpu.roll(tc, V7X_LANES - half, 1) * s1 + pltpu.roll(tc, half, 1) * s2
        outs.append(tr * scale if scale != 1.0 else tr)
    return jnp.concatenate(outs, axis=1)


def _replicate_heads(t, group):
    lane = lax.broadcasted_iota(jnp.int32, (t.shape[0], V7X_LANES), 1)
    first = lane < HEAD_DIM
    outs = []
    for j in range(t.shape[1] // V7X_LANES):
        tc = t[:, j * V7X_LANES:(j + 1) * V7X_LANES]
        tc_sw = pltpu.roll(tc, HEAD_DIM, 1)
        even = jnp.where(first, tc, tc_sw)
        odd = jnp.where(first, tc_sw, tc)
        outs += [even] * (group // 2) + [odd] * (group // 2)
    return jnp.concatenate(outs, axis=1)


def _qkv_kernel(x_ref, g_ref, w_ref, qg_ref, kg_ref, redq_ref, expq_ref, redk_ref, expk_ref,
                c_ref, s1_ref, s2_ref, q_out, k_out, v_out, *, q_dim, kv_dim, group):
    hn = _rms(x_ref[0], g_ref[...]).astype(BF16)
    c, s1, s2 = c_ref[...], s1_ref[...], s2_ref[...]
    q = _dot(hn, w_ref[0, :, 0:q_dim])
    q_out[0] = _norm_rope(q, redq_ref[...], expq_ref[...], qg_ref[...], c, s1, s2,
                          HEAD_DIM ** -0.5).astype(BF16)
    k = _dot(hn, w_ref[0, :, q_dim:q_dim + kv_dim])
    kr = _norm_rope(k, redk_ref[...], expk_ref[...], kg_ref[...], c, s1, s2, 1.0)
    k_out[0] = _replicate_heads(kr, group).astype(BF16)
    v = _dot(hn, w_ref[0, :, q_dim + kv_dim:q_dim + 2 * kv_dim])
    v_out[0] = _replicate_heads(v, group).astype(BF16)


def _attn_kernel(sink_ref, q_ref, kp_ref, k_ref, kn_ref, vp_ref, v_ref, vn_ref, x_ref, wo_ref,
                 o_ref, o_scr, *, tq, n_kv, group):
    i = pl.program_id(1)
    n = pl.num_programs(1)
    blk = ATTN_BLOCK
    gw = group * HEAD_DIM
    k_all = jnp.concatenate([kp_ref[0], k_ref[0], kn_ref[0]], axis=0)
    v_all = jnp.concatenate([vp_ref[0], v_ref[0], vn_ref[0]], axis=0)
    row = lax.broadcasted_iota(jnp.int32, (group * blk, 3 * blk), 0) % blk
    col = lax.broadcasted_iota(jnp.int32, (group * blk, 3 * blk), 1)
    band = (col >= row) & (col <= row + 2 * blk)
    head_of_lane = lax.broadcasted_iota(jnp.int32, (blk, gw), 1) // HEAD_DIM
    head_of_row = lax.broadcasted_iota(jnp.int32, (group * blk, 1), 0) // blk
    nblk = tq // blk
    for jb in range(nblk):
        valid = band
        if jb == 0:
            valid = valid & ((col >= blk) | (i > 0))
        if jb == nblk - 1:
            valid = valid & ((col < 2 * blk) | (i < n - 1))
        for kh in range(n_kv):
            qg = q_ref[0, jb * blk:(jb + 1) * blk, kh * gw:(kh + 1) * gw]
            kw = k_all[jb * blk:(jb + 3) * blk, kh * gw:(kh + 1) * gw]
            vw = v_all[jb * blk:(jb + 3) * blk, kh * gw:(kh + 1) * gw]
            qs = jnp.concatenate(
                [jnp.where(head_of_lane == hl, qg, jnp.zeros_like(qg)) for hl in range(group)], axis=0)
            sink = jnp.full((group * blk, 1), sink_ref[kh * group + group - 1], F32)
            for hl in range(group - 2, -1, -1):
                sink = jnp.where(head_of_row == hl, sink_ref[kh * group + hl], sink)
            s = jnp.where(valid, _dot_nt(qs, kw), -1e30)
            m = jnp.maximum(jnp.max(s, axis=-1, keepdims=True), sink)
            p = jnp.exp(s - m)
            denom = jnp.sum(p, axis=-1, keepdims=True) + jnp.exp(sink - m)
            o = _dot(p.astype(BF16), vw) / denom
            og = o[(group - 1) * blk:group * blk]
            for hl in range(group - 2, -1, -1):
                og = jnp.where(head_of_lane == hl, o[hl * blk:(hl + 1) * blk], og)
            o_scr[jb * blk:(jb + 1) * blk, kh * gw:(kh + 1) * gw] = og.astype(BF16)
    o_ref[0] = x_ref[0] + _dot(o_scr[...], wo_ref[0])


def _window_attention(x, norm_g, w_qkv, q_g, k_g, sink, w_o, layer, rope, *, tm=1024, tq=512):
    bsz, s_len, d = x.shape
    n_heads = sink.shape[0]
    q_dim = w_o.shape[1]
    kv_dim = (w_qkv.shape[2] - q_dim) // 2
    n_kv = kv_dim // HEAD_DIM
    group = n_heads // n_kv
    tm = min(tm, s_len)
    tq = min(tq, s_len)
    c, s1, s2 = rope

    def reduce_expand(width):
        i = lax.broadcasted_iota(jnp.int32, (width, V7X_LANES), 0)
        j = lax.broadcasted_iota(jnp.int32, (width, V7X_LANES), 1)
        red = jnp.where(i // HEAD_DIM == j, 1.0 / HEAD_DIM, 0.0).astype(BF16)
        expd = ((j.T % 16 == i.T // HEAD_DIM) & (j.T < 32)).astype(BF16)
        return red, expd

    redq, expq = reduce_expand(q_dim)
    redk, expk = reduce_expand(kv_dim)
    tile = lambda w: pl.BlockSpec((1, tm, w), lambda b, i: (b, i, 0))
    rope_spec = pl.BlockSpec((tm, V7X_LANES), lambda b, i: (i, 0))
    q, k, v = pl.pallas_call(
        functools.partial(_qkv_kernel, q_dim=q_dim, kv_dim=kv_dim, group=group),
        grid=(bsz, s_len // tm),
        in_specs=[tile(d), _full((1, d)), _layer(w_qkv, layer), _full((1, q_dim)),
                  _full((1, kv_dim)), _full(redq.shape), _full(expq.shape), _full(redk.shape),
                  _full(expk.shape), rope_spec, rope_spec, rope_spec],
        out_specs=[tile(q_dim), tile(q_dim), tile(q_dim)],
        out_shape=[jax.ShapeDtypeStruct((bsz, s_len, q_dim), BF16)] * 3,
        compiler_params=pltpu.CompilerParams(
            dimension_semantics=("parallel", "parallel"), vmem_limit_bytes=V7X_VMEM_LIMIT),
        name="attn_qkv",
    )(x, norm_g.reshape(1, d), w_qkv, jnp.tile(q_g, n_heads).reshape(1, q_dim),
      jnp.tile(k_g, n_kv).reshape(1, kv_dim), redq, expq, redk, expk, c, s1, s2)

    r = tq // ATTN_BLOCK
    last = s_len // ATTN_BLOCK - 1
    prev_spec = pl.BlockSpec((1, ATTN_BLOCK, q_dim), lambda b, i: (b, jnp.maximum(i * r - 1, 0), 0))
    main_spec = pl.BlockSpec((1, tq, q_dim), lambda b, i: (b, i, 0))
    next_spec = pl.BlockSpec((1, ATTN_BLOCK, q_dim), lambda b, i: (b, jnp.minimum((i + 1) * r, last), 0))
    return pl.pallas_call(
        functools.partial(_attn_kernel, tq=tq, n_kv=n_kv, group=group),
        grid=(bsz, s_len // tq),
        in_specs=[pl.BlockSpec(memory_space=pltpu.SMEM), main_spec,
                  prev_spec, main_spec, next_spec, prev_spec, main_spec, next_spec,
                  pl.BlockSpec((1, tq, d), lambda b, i: (b, i, 0)), _layer(w_o, layer)],
        out_specs=pl.BlockSpec((1, tq, d), lambda b, i: (b, i, 0)),
        out_shape=jax.ShapeDtypeStruct(x.shape, F32),
        scratch_shapes=[pltpu.VMEM((tq, q_dim), BF16)],
        compiler_params=pltpu.CompilerParams(
            dimension_semantics=("parallel", "parallel"), vmem_limit_bytes=V7X_VMEM_LIMIT),
        name="attn_core",
    )(sink.astype(F32), q, k, k, k, v, v, v, x, w_o)


def _softplus(x):
    return jnp.maximum(x, 0.0) + jnp.log1p(jnp.exp(-jnp.abs(x)))


def _ssd_in_kernel(xp_ref, x_ref, xn_ref, g_ref, w_ref, wdt_ref, cw_ref, cb_ref, dtb_ref,
                   z_out, xbc_out, dt_out, hn_scr, perm_scr, *, tm, fc, d_inner, conv_dim):
    g = g_ref[...]
    _fill_normed(hn_scr, perm_scr, xp_ref, x_ref, xn_ref, g, tm)
    hn_nat = _rms(x_ref[0], g).astype(BF16)
    z_out[0] = _silu(_dot(hn_nat, w_ref[0, :, 0:d_inner])).astype(BF16)
    dt_out[0] = _softplus(_dot(hn_nat, wdt_ref[...]) + dtb_ref[...])
    hn = hn_scr[...]
    for c in range(conv_dim // fc):
        lo, hi = c * fc, (c + 1) * fc
        h = _dot(hn, w_ref[0, :, d_inner + lo:d_inner + hi])
        y = _unpermute(perm_scr, _dwconv(h, cw_ref[:, lo:hi], cb_ref[:, lo:hi], tm), tm)
        xbc_out[0, :, lo:hi] = _silu(y).astype(BF16)


def _pieces(x, n):
    out = []
    for _ in range(n - 1):
        p = x.astype(BF16).astype(F32)
        out.append(p)
        x = x - p
    out.append(x.astype(BF16).astype(F32))
    return out


def _ssd_decay_kernel(dt_ref, a_ref, at_ref, acum_out, tr_out, ew_out, *, heads, cps):
    lc = SSD_CHUNK
    row = lax.broadcasted_iota(jnp.int32, (lc, lc), 0)
    col = lax.broadcasted_iota(jnp.int32, (lc, lc), 1)
    lower = (row >= col).astype(BF16)
    upper = (row <= col).astype(BF16)
    fwd_lane = lax.broadcasted_iota(jnp.int32, (lc, V7X_LANES), 1) < heads
    fwd_row = lax.broadcasted_iota(jnp.int32, (V7X_LANES, lc), 0) < heads
    for k in range(cps):
        rows = slice(k * lc, (k + 1) * lc)
        dt = dt_ref[0, rows, :]
        dtt = dt.T
        ps = [p.astype(BF16) for p in _pieces(dt * a_ref[...], 3)]
        cum_f = (_dot(lower, ps[0]) + _dot(lower, ps[1])) + _dot(lower, ps[2])
        cum_b = (_dot(upper, ps[0]) + _dot(upper, ps[1])) + _dot(upper, ps[2])
        a_cum = jnp.where(fwd_lane, cum_f, cum_b)
        qs = [q.astype(BF16) for q in _pieces(dtt * at_ref[...], 3)]
        cum_tf = (_dot(qs[0], upper) + _dot(qs[1], upper)) + _dot(qs[2], upper)
        cum_tb = (_dot(qs[0], lower) + _dot(qs[1], lower)) + _dot(qs[2], lower)
        a_cum_t = jnp.where(fwd_row, cum_tf, cum_tb)
        a_end = jnp.where(fwd_lane[0:1], a_cum[lc - 1:lc, :], a_cum[0:1, :])
        e1, e2 = _pieces(jnp.exp(a_cum), 2)
        w1, w2 = _pieces(dt * jnp.exp(a_end - a_cum), 2)
        acum_out[0, rows, :] = a_cum
        half = V7X_LANES // 2
        dsum = jnp.log(dtt[0:heads] + dtt[heads:2 * heads])
        tr_out[0, :, rows] = jnp.concatenate(
            [a_cum_t[0:half] - jnp.log(dtt[0:half]), dsum, jnp.zeros((half - heads, lc), F32)], axis=0)
        for d in range(2):
            lanes = slice(d * heads, (d + 1) * heads)
            ew_out[0, rows, d * V7X_LANES:(d + 1) * V7X_LANES] = jnp.concatenate(
                [e1[:, lanes], e2[:, lanes], w1[:, lanes], w2[:, lanes]], axis=1).astype(BF16)


def _ssd_state_step(xbc_ref, ew, state_ref, rows, g, end, d_inner):
    gw = d_inner // SSD_GROUPS
    gn = SSD_GROUPS * D_STATE
    e_exp = ew[:, g * gw:(g + 1) * gw]
    w_exp = ew[:, d_inner + g * gw:d_inner + (g + 1) * gw]
    b_g = xbc_ref[0, rows, d_inner + g * D_STATE:d_inner + (g + 1) * D_STATE]
    c_g = xbc_ref[0, rows, d_inner + gn + g * D_STATE:d_inner + gn + (g + 1) * D_STATE]
    x_g = xbc_ref[0, rows, g * gw:(g + 1) * gw]
    st = state_ref[g]
    y_off = _dot(c_g, st.astype(BF16)) * e_exp
    xw = (x_g.astype(F32) * w_exp).astype(BF16)
    state_ref[g] = e_exp[end:end + 1, :] * st + _dot_tn(b_g, xw)
    return y_off, b_g, c_g, x_g


def _ssd_intra(acum_ref, tr_ref, rows, g, b_g, c_g, x_g, heads, masks):
    from_fwd, from_bwd, lane_head = masks
    hpg = heads // SSD_GROUPS
    cb = _dot_nt(c_g, b_g)
    ms, xb = [], []
    for r in range(hpg):
        h = g * hpg + r
        zf = acum_ref[0, rows, h:h + 1] - tr_ref[0, h:h + 1, rows]
        zb = acum_ref[0, rows, heads + h:heads + h + 1] - tr_ref[0, heads + h:heads + h + 1, rows]
        zd = tr_ref[0, 2 * heads + h:2 * heads + h + 1, rows]
        z = jnp.where(from_fwd, zf, jnp.where(from_bwd, zb, zd))
        ms.append((cb * jnp.exp(z)).astype(BF16))
        xb.append(jnp.where(lane_head == r, x_g, jnp.zeros_like(x_g)))
    return _dot(jnp.concatenate(ms, axis=1), jnp.concatenate(xb, axis=0))


def _intra_masks(lc, gw):
    row = lax.broadcasted_iota(jnp.int32, (lc, lc), 0)
    col = lax.broadcasted_iota(jnp.int32, (lc, lc), 1)
    lane_head = lax.broadcasted_iota(jnp.int32, (lc, gw), 1) // SSD_HEAD_DIM
    return col < row, col > row, lane_head


def _ssd_bwd_kernel(xbc_ref, acum_ref, tr_ref, ew_ref, xsel_ref, y_out, state_ref,
                    *, d_inner, heads, cps, intra_groups):
    @pl.when(pl.program_id(1) == 0)
    def _():
        state_ref[...] = jnp.zeros_like(state_ref)

    lc = SSD_CHUNK
    gw = d_inner // SSD_GROUPS
    masks = _intra_masks(lc, gw)
    for k in range(cps - 1, -1, -1):
        rows = slice(k * lc, (k + 1) * lc)
        ew = _dot(ew_ref[0, rows, V7X_LANES:2 * V7X_LANES], xsel_ref[...])
        for g in range(SSD_GROUPS):
            y, b_g, c_g, x_g = _ssd_state_step(xbc_ref, ew, state_ref, rows, g, 0, d_inner)
            if g in intra_groups:
                y = y + _ssd_intra(acum_ref, tr_ref, rows, g, b_g, c_g, x_g, heads, masks)
            y_out[0, rows, g * gw:(g + 1) * gw] = y


def _ssd_fwd_kernel(xbc_ref, acum_ref, tr_ref, ew_ref, xsel_ref, yb_ref, z_ref, x_ref,
                    dexp_ref, gg_ref, wout_ref, o_ref, state_ref, yn_scr,
                    *, d_inner, heads, cps, intra_groups):
    @pl.when(pl.program_id(1) == 0)
    def _():
        state_ref[...] = jnp.zeros_like(state_ref)

    gw = d_inner // SSD_GROUPS
    lc = SSD_CHUNK
    masks = _intra_masks(lc, gw)
    for k in range(cps):
        rows = slice(k * lc, (k + 1) * lc)
        ew = _dot(ew_ref[0, rows, 0:V7X_LANES], xsel_ref[...])
        for g in range(SSD_GROUPS):
            cols = slice(g * gw, (g + 1) * gw)
            y, b_g, c_g, x_g = _ssd_state_step(xbc_ref, ew, state_ref, rows, g, lc - 1, d_inner)
            if g in intra_groups:
                y = y + _ssd_intra(acum_ref, tr_ref, rows, g, b_g, c_g, x_g, heads, masks)
            y = y + yb_ref[0, rows, cols] + x_g.astype(F32) * dexp_ref[:, cols]
            y = y * z_ref[0, rows, cols].astype(F32)
            y = y * lax.rsqrt(jnp.mean(y * y, axis=-1, keepdims=True) + EPS)
            yn_scr[rows, cols] = (y * gg_ref[:, cols]).astype(BF16)
    o_ref[0] = x_ref[0] + _dot(yn_scr[...], wout_ref[0])


def _ssd_mixer(x, norm_g, w_in, conv_w, conv_b, dt_bias, a_log, d_skip, gate_g, w_out, layer,
               *, tm=512, fc=512, cps=4):
    bsz, s_len, d = x.shape
    d_inner = w_out.shape[1]
    heads = d_skip.shape[0]
    conv_dim = conv_w.shape[1]
    tm = min(tm, s_len)
    dt_pad = V7X_LANES - 2 * heads
    w_dt = jnp.pad(w_in[layer, :, d_inner + conv_dim:], ((0, 0), (0, dt_pad)))
    dt_b = jnp.pad(dt_bias.reshape(1, 2 * heads).astype(F32), ((0, 0), (0, dt_pad)))
    tile = lambda w: pl.BlockSpec((1, tm, w), lambda b, i: (b, i, 0))
    sz, xbc, dt = pl.pallas_call(
        functools.partial(_ssd_in_kernel, tm=tm, fc=fc, d_inner=d_inner, conv_dim=conv_dim),
        grid=(bsz, s_len // tm),
        in_specs=_halo_specs(tm, s_len, d) + [
            _full((1, d)), _layer(w_in, layer), _full((d, V7X_LANES)), _full(conv_w.shape),
            _full((1, conv_dim)), _full((1, V7X_LANES))],
        out_specs=[tile(d_inner), tile(conv_dim), tile(V7X_LANES)],
        out_shape=[jax.ShapeDtypeStruct((bsz, s_len, d_inner), BF16),
                   jax.ShapeDtypeStruct((bsz, s_len, conv_dim), BF16),
                   jax.ShapeDtypeStruct((bsz, s_len, V7X_LANES), F32)],
        scratch_shapes=[pltpu.VMEM((tm + 2 * F32_SUBLANES, d), BF16),
                        pltpu.VMEM((d // V7X_LANES, tm, V7X_LANES), F32)],
        compiler_params=pltpu.CompilerParams(
            dimension_semantics=("parallel", "parallel"), vmem_limit_bytes=V7X_VMEM_LIMIT),
        name="ssd_in",
    )(x, x, x, norm_g.reshape(1, d), w_in, w_dt, conv_w, conv_b.reshape(1, conv_dim), dt_b)

    lc = SSD_CHUNK
    cps = min(cps, s_len // lc)
    nc = s_len // (lc * cps)
    a_lanes = jnp.pad((-jnp.exp(a_log.astype(F32))).reshape(1, 2 * heads), ((0, 0), (0, dt_pad)))
    step = lambda w: pl.BlockSpec((1, cps * lc, w), lambda b, c: (b, c, 0))
    acum, tr, ew = pl.pallas_call(
        functools.partial(_ssd_decay_kernel, heads=heads, cps=cps),
        grid=(bsz, nc),
        in_specs=[step(V7X_LANES), _full((1, V7X_LANES)), _full((V7X_LANES, 1))],
        out_specs=[step(V7X_LANES), pl.BlockSpec((1, V7X_LANES, cps * lc), lambda b, c: (b, 0, c)),
                   step(2 * V7X_LANES)],
        out_shape=[jax.ShapeDtypeStruct((bsz, s_len, V7X_LANES), F32),
                   jax.ShapeDtypeStruct((bsz, V7X_LANES, s_len), F32),
                   jax.ShapeDtypeStruct((bsz, s_len, 2 * V7X_LANES), BF16)],
        compiler_params=pltpu.CompilerParams(
            dimension_semantics=("parallel", "parallel"), vmem_limit_bytes=V7X_VMEM_LIMIT),
        name="ssd_decay",
    )(dt, a_lanes, a_lanes.reshape(V7X_LANES, 1))

    xr = lax.broadcasted_iota(jnp.int32, (4 * heads, 2 * d_inner), 0)
    xc = lax.broadcasted_iota(jnp.int32, (4 * heads, 2 * d_inner), 1)
    xsel = ((xr % heads == (xc % d_inner) // SSD_HEAD_DIM)
            & (xr // (2 * heads) == xc // d_inner)).astype(BF16)
    bwd_share = 4
    state = pltpu.VMEM((SSD_GROUPS, D_STATE, d_inner // SSD_GROUPS), F32)
    rev = lambda w: pl.BlockSpec((1, cps * lc, w), lambda b, c: (b, nc - 1 - c, 0))
    fwd = step
    y_b = pl.pallas_call(
        functools.partial(_ssd_bwd_kernel, d_inner=d_inner, heads=heads, cps=cps,
                          intra_groups=range(0, bwd_share)),
        grid=(bsz, nc),
        in_specs=[rev(conv_dim), rev(V7X_LANES),
                  pl.BlockSpec((1, V7X_LANES, cps * lc), lambda b, c: (b, 0, nc - 1 - c)),
                  rev(2 * V7X_LANES), _full(xsel.shape)],
        out_specs=rev(d_inner),
        out_shape=jax.ShapeDtypeStruct((bsz, s_len, d_inner), F32),
        scratch_shapes=[state],
        compiler_params=pltpu.CompilerParams(
            dimension_semantics=("parallel", "arbitrary"), vmem_limit_bytes=V7X_VMEM_LIMIT),
        name="ssd_scan_bwd",
    )(xbc, acum, tr, ew, xsel)

    return pl.pallas_call(
        functools.partial(_ssd_fwd_kernel, d_inner=d_inner, heads=heads, cps=cps,
                          intra_groups=range(bwd_share, SSD_GROUPS)),
        grid=(bsz, nc),
        in_specs=[fwd(conv_dim), fwd(V7X_LANES),
                  pl.BlockSpec((1, V7X_LANES, cps * lc), lambda b, c: (b, 0, c)),
                  fwd(2 * V7X_LANES), _full(xsel.shape), fwd(d_inner), fwd(d_inner), fwd(d),
                  _full((1, d_inner)), _full((1, d_inner)), _layer(w_out, layer)],
        out_specs=fwd(d),
        out_shape=jax.ShapeDtypeStruct(x.shape, F32),
        scratch_shapes=[state, pltpu.VMEM((cps * lc, d_inner), BF16)],
        compiler_params=pltpu.CompilerParams(
            dimension_semantics=("parallel", "arbitrary"), vmem_limit_bytes=V7X_VMEM_LIMIT),
        name="ssd_scan_fwd",
    )(xbc, acum, tr, ew, xsel, y_b, sz, x,
      jnp.broadcast_to(d_skip.astype(F32)[:, None], (heads, SSD_HEAD_DIM)).reshape(1, d_inner),
      gate_g.reshape(1, d_inner).astype(F32), w_out)


def kernel(x, attn_norm, attn_w_qkv, attn_q_norm, attn_k_norm, attn_sink, attn_w_o, ssd_norm, ssd_w_in, ssd_conv_w, ssd_conv_b, ssd_dt_bias, ssd_a_log, ssd_d, ssd_gate_norm, ssd_w_out, ffn_norm, ffn_w_up, ffn_conv_w, ffn_conv_b, ffn_w_down):
    depth = ffn_norm.shape[0]
    rope = _rope_tables(x.shape[1])
    w_qkv, w_o = attn_w_qkv.astype(BF16), attn_w_o.astype(BF16)
    w_in, w_out = ssd_w_in.astype(BF16), ssd_w_out.astype(BF16)
    w_up, w_down = ffn_w_up.astype(BF16), ffn_w_down.astype(BF16)
    for i in range(depth):
        j = i // 2
        if i % 2 == 0:
            x = _window_attention(x, attn_norm[j], w_qkv, attn_q_norm[j], attn_k_norm[j],
                                  attn_sink[j], w_o, j, rope)
        else:
            x = _ssd_mixer(x, ssd_norm[j], w_in, ssd_conv_w[j], ssd_conv_b[j], ssd_dt_bias[j],
                           ssd_a_log[j], ssd_d[j], ssd_gate_norm[j], w_out, j)
        x = _conv_ffn(x, ffn_norm[i], w_up, ffn_conv_w[i], ffn_conv_b[i], w_down, i)
    return x
```

```python
import functools

import jax
import jax.numpy as jnp
from jax import lax
from jax.experimental import pallas as pl
from jax.experimental.pallas import tpu as pltpu

F32 = jnp.float32
BF16 = jnp.bfloat16

EPS = 1e-6
HEAD_DIM = 64
ROT_DIM = HEAD_DIM // 4
ROPE_THETA = 500000.0
ATTN_BLOCK = 128
SSD_HEAD_DIM = 64
SSD_GROUPS = 8
D_STATE = 128
SSD_CHUNK = 128

V7X_LANES = 128
F32_SUBLANES = 8
V7X_VMEM_BYTES = 64 * 1024 * 1024
V7X_VMEM_LIMIT = V7X_VMEM_BYTES * 7 // 8
NORM_HEAD_SLOTS = 16

FFN_TILE, FFN_COLS = 1024, 256
QKV_TILE, ATTN_TILE = 1024, 512
SSD_IN_TILE, SSD_IN_COLS = 512, 512
SSD_CHUNKS_PER_STEP = 4
SSD_BWD_INTRA_GROUPS = 4


def _rms(x, g):
    return x * lax.rsqrt(jnp.mean(x * x, axis=-1, keepdims=True) + EPS) * g


def _silu(x):
    h = 0.5 * x
    return h + h * jnp.tanh(h)


def _dot(a, b):
    return jnp.dot(a, b, preferred_element_type=F32)


def _dot_nt(a, b):
    return lax.dot_general(a, b, (((1,), (1,)), ((), ())), preferred_element_type=F32)


def _dot_tn(a, b):
    return lax.dot_general(a, b, (((0,), (0,)), ((), ())), preferred_element_type=F32)


def _halo_specs(tm, s_len, d):
    r = tm // F32_SUBLANES
    last = s_len // F32_SUBLANES - 1
    return [
        pl.BlockSpec((1, F32_SUBLANES, d), lambda b, i: (b, jnp.maximum(i * r - 1, 0), 0)),
        pl.BlockSpec((1, tm, d), lambda b, i: (b, i, 0)),
        pl.BlockSpec((1, F32_SUBLANES, d), lambda b, i: (b, jnp.minimum((i + 1) * r, last), 0)),
    ]


def _full(shape):
    return pl.BlockSpec(shape, lambda b, i: (0,) * len(shape), pipeline_mode=pl.Buffered(1))


def _layer(stacked, j):
    return pl.BlockSpec((1,) + stacked.shape[1:], lambda b, i: (j, 0, 0), pipeline_mode=pl.Buffered(1))


def _perm_base(a, tm):
    p = tm // F32_SUBLANES
    t = a * F32_SUBLANES
    return (t % p) * F32_SUBLANES + t // p


def _fill_normed(hn_scr, perm_scr, xp_ref, x_ref, xn_ref, g, tm):
    i = pl.program_id(1)
    n = pl.num_programs(1)
    h = F32_SUBLANES
    keep_p = (i > 0).astype(F32)
    keep_n = (i < n - 1).astype(F32)
    halo = jnp.concatenate([_rms(xp_ref[0], g) * keep_p, _rms(xn_ref[0], g) * keep_n], axis=0)
    hn_scr[0:2 * h, :] = halo.astype(BF16)
    xn = _rms(x_ref[0], g)
    slabs = xn.shape[1] // V7X_LANES
    for a in range(tm // h):
        for j in range(slabs):
            perm_scr[j, pl.ds(_perm_base(a, tm), h, stride=h), :] = (
                xn[a * h:(a + 1) * h, j * V7X_LANES:(j + 1) * V7X_LANES])
    for j in range(slabs):
        hn_scr[2 * h:2 * h + tm, j * V7X_LANES:(j + 1) * V7X_LANES] = perm_scr[j].astype(BF16)


def _unpermute(perm_scr, y, tm):
    h = F32_SUBLANES
    slabs = y.shape[1] // V7X_LANES
    for j in range(slabs):
        perm_scr[j] = y[:, j * V7X_LANES:(j + 1) * V7X_LANES]
    rows = []
    for a in range(tm // h):
        rows.append(jnp.concatenate(
            [perm_scr[j, pl.ds(_perm_base(a, tm), h, stride=h), :] for j in range(slabs)], axis=1))
    return jnp.concatenate(rows, axis=0)


def _dwconv(hh, w, b, tm):
    h = F32_SUBLANES
    k_w = w.shape[0]
    pad = k_w // 2
    hp, hx, hm = hh[0:h], hh[h:2 * h], hh[2 * h:]
    sub = lax.broadcasted_iota(jnp.int32, hp.shape, 0)
    before = []
    for e in range(pad, 0, -1):
        src = jnp.where(sub == h - 1, pltpu.roll(hp, e - 1, 0) if e > 1 else hp, hm[tm - e * h:tm - (e - 1) * h])
        before.append(pltpu.roll(src, 1, 0))
    after = []
    for e in range(pad):
        src = jnp.where(sub == 0, pltpu.roll(hx, h - e, 0) if e > 0 else hx, hm[e * h:(e + 1) * h])
        after.append(pltpu.roll(src, h - 1, 0))
    ext = jnp.concatenate(before + [hm] + after, axis=0)
    y = b + ext[0:tm] * w[0:1]
    for k in range(1, k_w):
        y = y + ext[k * h:k * h + tm] * w[k:k + 1]
    return y


def _ffn_kernel(xp_ref, x_ref, xn_ref, g_ref, wup_ref, cw_ref, cb_ref, wdn_ref, o_ref, hn_scr,
                perm_scr, act_scr, *, tm, fc, d_ff):
    _fill_normed(hn_scr, perm_scr, xp_ref, x_ref, xn_ref, g_ref[...], tm)
    hn = hn_scr[...]
    for c in range(d_ff // fc):
        lo, hi = c * fc, (c + 1) * fc
        hg = _dot(hn, wup_ref[0, :, lo:hi])
        hv = _dot(hn, wup_ref[0, :, d_ff + lo:d_ff + hi])
        gate = _dwconv(hg, cw_ref[:, lo:hi], cb_ref[:, lo:hi], tm)
        val = _dwconv(hv, cw_ref[:, d_ff + lo:d_ff + hi], cb_ref[:, d_ff + lo:d_ff + hi], tm)
        act_scr[:, lo:hi] = (_silu(gate) * val).astype(BF16)
    o_ref[0] = x_ref[0] + _unpermute(perm_scr, _dot(act_scr[...], wdn_ref[0]), tm)


def _conv_ffn(x, g, w_up, conv_w, conv_b, w_down, layer, *, tm=FFN_TILE, fc=FFN_COLS):
    bsz, s_len, d = x.shape
    d_ff = w_down.shape[1]
    tm = min(tm, s_len)
    assert s_len % tm == 0 and tm % (F32_SUBLANES * F32_SUBLANES) == 0 and d_ff % fc == 0
    kern = functools.partial(_ffn_kernel, tm=tm, fc=fc, d_ff=d_ff)
    return pl.pallas_call(
        kern,
        grid=(bsz, s_len // tm),
        in_specs=_halo_specs(tm, s_len, d) + [
            _full((1, d)), _layer(w_up, layer), _full(conv_w.shape), _full((1, 2 * d_ff)),
            _layer(w_down, layer)],
        out_specs=pl.BlockSpec((1, tm, d), lambda b, i: (b, i, 0)),
        out_shape=jax.ShapeDtypeStruct(x.shape, F32),
        scratch_shapes=[pltpu.VMEM((tm + 2 * F32_SUBLANES, d), BF16),
                        pltpu.VMEM((d // V7X_LANES, tm, V7X_LANES), F32),
                        pltpu.VMEM((tm, d_ff), BF16)],
        compiler_params=pltpu.CompilerParams(
            dimension_semantics=("parallel", "parallel"), vmem_limit_bytes=V7X_VMEM_LIMIT),
        name="conv_ffn",
    )(x, x, x, g.reshape(1, d), w_up, conv_w, conv_b.reshape(1, -1), w_down)


def _rope_tables(s_len):
    half = ROT_DIM // 2
    pos = jnp.arange(s_len, dtype=F32)
    inv_freq = ROPE_THETA ** (-(jnp.arange(0, ROT_DIM, 2, dtype=F32) / ROT_DIM))
    ang = pos[:, None] * inv_freq[None, :]
    cos, sin = jnp.cos(ang), jnp.sin(ang)
    rest = HEAD_DIM - ROT_DIM
    c = jnp.concatenate([cos, cos, jnp.ones((s_len, rest), F32)], axis=1)
    s1 = jnp.concatenate([-sin, jnp.zeros((s_len, half + rest), F32)], axis=1)
    s2 = jnp.concatenate([jnp.zeros((s_len, half), F32), sin, jnp.zeros((s_len, rest), F32)], axis=1)
    rep = V7X_LANES // HEAD_DIM
    return jnp.tile(c, (1, rep)), jnp.tile(s1, (1, rep)), jnp.tile(s2, (1, rep))


def _norm_rope(t, red, expd, g, c, s1, s2, scale):
    half = ROT_DIM // 2
    rs = lax.rsqrt(_dot((t * t).astype(BF16), red) + EPS)
    hi = rs.astype(BF16).astype(F32)
    lane = lax.broadcasted_iota(jnp.int32, rs.shape, 1)
    packed = jnp.where(lane < NORM_HEAD_SLOTS, hi, pltpu.roll(rs - hi, NORM_HEAD_SLOTS, 1))
    tn = t * _dot(packed.astype(BF16), expd) * g
    outs = []
    for j in range(t.shape[1] // V7X_LANES):
        tc = tn[:, j * V7X_LANES:(j + 1) * V7X_LANES]
        tr = tc * c + pltpu.roll(tc, V7X_LANES - half, 1) * s1 + pltpu.roll(tc, half, 1) * s2
        outs.append(tr * scale if scale != 1.0 else tr)
    return jnp.concatenate(outs, axis=1)


def _replicate_heads(t, group):
    lane = lax.broadcasted_iota(jnp.int32, (t.shape[0], V7X_LANES), 1)
    first = lane < HEAD_DIM
    outs = []
    for j in range(t.shape[1] // V7X_LANES):
        tc = t[:, j * V7X_LANES:(j + 1) * V7X_LANES]
        tc_sw = pltpu.roll(tc, HEAD_DIM, 1)
        even = jnp.where(first, tc, tc_sw)
        odd = jnp.where(first, tc_sw, tc)
        outs += [even] * (group // 2) + [odd] * (group // 2)
    return jnp.concatenate(outs, axis=1)


def _qkv_kernel(x_ref, g_ref, w_ref, qg_ref, kg_ref, redq_ref, expq_ref, redk_ref, expk_ref,
                c_ref, s1_ref, s2_ref, q_out, k_out, v_out, *, q_dim, kv_dim, group):
    hn = _rms(x_ref[0], g_ref[...]).astype(BF16)
    c, s1, s2 = c_ref[...], s1_ref[...], s2_ref[...]
    q = _dot(hn, w_ref[0, :, 0:q_dim])
    q_out[0] = _norm_rope(q, redq_ref[...], expq_ref[...], qg_ref[...], c, s1, s2,
                          HEAD_DIM ** -0.5).astype(BF16)
    k = _dot(hn, w_ref[0, :, q_dim:q_dim + kv_dim])
    kr = _norm_rope(k, redk_ref[...], expk_ref[...], kg_ref[...], c, s1, s2, 1.0)
    k_out[0] = _replicate_heads(kr, group).astype(BF16)
    v = _dot(hn, w_ref[0, :, q_dim + kv_dim:q_dim + 2 * kv_dim])
    v_out[0] = _replicate_heads(v, group).astype(BF16)


def _attn_kernel(sink_ref, q_ref, kp_ref, k_ref, kn_ref, vp_ref, v_ref, vn_ref, x_ref, wo_ref,
                 o_ref, o_scr, *, tq, n_kv, group):
    i = pl.program_id(1)
    n = pl.num_programs(1)
    blk = ATTN_BLOCK
    gw = group * HEAD_DIM
    k_all = jnp.concatenate([kp_ref[0], k_ref[0], kn_ref[0]], axis=0)
    v_all = jnp.concatenate([vp_ref[0], v_ref[0], vn_ref[0]], axis=0)
    row = lax.broadcasted_iota(jnp.int32, (group * blk, 3 * blk), 0) % blk
    col = lax.broadcasted_iota(jnp.int32, (group * blk, 3 * blk), 1)
    band = (col >= row) & (col <= row + 2 * blk)
    head_of_lane = lax.broadcasted_iota(jnp.int32, (blk, gw), 1) // HEAD_DIM
    head_of_row = lax.broadcasted_iota(jnp.int32, (group * blk, 1), 0) // blk
    nblk = tq // blk
    for jb in range(nblk):
        valid = band
        if jb == 0:
            valid = valid & ((col >= blk) | (i > 0))
        if jb == nblk - 1:
            valid = valid & ((col < 2 * blk) | (i < n - 1))
        for kh in range(n_kv):
            qg = q_ref[0, jb * blk:(jb + 1) * blk, kh * gw:(kh + 1) * gw]
            kw = k_all[jb * blk:(jb + 3) * blk, kh * gw:(kh + 1) * gw]
            vw = v_all[jb * blk:(jb + 3) * blk, kh * gw:(kh + 1) * gw]
            qs = jnp.concatenate(
                [jnp.where(head_of_lane == hl, qg, jnp.zeros_like(qg)) for hl in range(group)], axis=0)
            sink = jnp.full((group * blk, 1), sink_ref[kh * group + group - 1], F32)
            for hl in range(group - 2, -1, -1):
                sink = jnp.where(head_of_row == hl, sink_ref[kh * group + hl], sink)
            s = jnp.where(valid, _dot_nt(qs, kw), -1e30)
            m = jnp.maximum(jnp.max(s, axis=-1, keepdims=True), sink)
            p = jnp.exp(s - m)
            denom = jnp.sum(p, axis=-1, keepdims=True) + jnp.exp(sink - m)
            o = _dot(p.astype(BF16), vw) / denom
            og = o[(group - 1) * blk:group * blk]
            for hl in range(group - 2, -1, -1):
                og = jnp.where(head_of_lane == hl, o[hl * blk:(hl + 1) * blk], og)
            o_scr[jb * blk:(jb + 1) * blk, kh * gw:(kh + 1) * gw] = og.astype(BF16)
    o_ref[0] = x_ref[0] + _dot(o_scr[...], wo_ref[0])


def _window_attention(x, norm_g, w_qkv, q_g, k_g, sink, w_o, layer, rope,
                      *, tm=QKV_TILE, tq=ATTN_TILE):
    bsz, s_len, d = x.shape
    n_heads = sink.shape[0]
    q_dim = w_o.shape[1]
    kv_dim = (w_qkv.shape[2] - q_dim) // 2
    n_kv = kv_dim // HEAD_DIM
    group = n_heads // n_kv
    tm = min(tm, s_len)
    tq = min(tq, s_len)
    assert s_len % tm == 0 and s_len % tq == 0 and tq % ATTN_BLOCK == 0
    assert n_heads <= NORM_HEAD_SLOTS and group % 2 == 0 and q_dim == n_heads * HEAD_DIM
    c, s1, s2 = rope

    def reduce_expand(width):
        i = lax.broadcasted_iota(jnp.int32, (width, V7X_LANES), 0)
        j = lax.broadcasted_iota(jnp.int32, (width, V7X_LANES), 1)
        red = jnp.where(i // HEAD_DIM == j, 1.0 / HEAD_DIM, 0.0).astype(BF16)
        expd = ((j.T % NORM_HEAD_SLOTS == i.T // HEAD_DIM) & (j.T < 2 * NORM_HEAD_SLOTS)).astype(BF16)
        return red, expd

    redq, expq = reduce_expand(q_dim)
    redk, expk = reduce_expand(kv_dim)
    tile = lambda w: pl.BlockSpec((1, tm, w), lambda b, i: (b, i, 0))
    rope_spec = pl.BlockSpec((tm, V7X_LANES), lambda b, i: (i, 0))
    q, k, v = pl.pallas_call(
        functools.partial(_qkv_kernel, q_dim=q_dim, kv_dim=kv_dim, group=group),
        grid=(bsz, s_len // tm),
        in_specs=[tile(d), _full((1, d)), _layer(w_qkv, layer), _full((1, q_dim)),
                  _full((1, kv_dim)), _full(redq.shape), _full(expq.shape), _full(redk.shape),
                  _full(expk.shape), rope_spec, rope_spec, rope_spec],
        out_specs=[tile(q_dim), tile(q_dim), tile(q_dim)],
        out_shape=[jax.ShapeDtypeStruct((bsz, s_len, q_dim), BF16)] * 3,
        compiler_params=pltpu.CompilerParams(
            dimension_semantics=("parallel", "parallel"), vmem_limit_bytes=V7X_VMEM_LIMIT),
        name="attn_qkv",
    )(x, norm_g.reshape(1, d), w_qkv, jnp.tile(q_g, n_heads).reshape(1, q_dim),
      jnp.tile(k_g, n_kv).reshape(1, kv_dim), redq, expq, redk, expk, c, s1, s2)

    r = tq // ATTN_BLOCK
    last = s_len // ATTN_BLOCK - 1
    prev_spec = pl.BlockSpec((1, ATTN_BLOCK, q_dim), lambda b, i: (b, jnp.maximum(i * r - 1, 0), 0))
    main_spec = pl.BlockSpec((1, tq, q_dim), lambda b, i: (b, i, 0))
    next_spec = pl.BlockSpec((1, ATTN_BLOCK, q_dim), lambda b, i: (b, jnp.minimum((i + 1) * r, last), 0))
    return pl.pallas_call(
        functools.partial(_attn_kernel, tq=tq, n_kv=n_kv, group=group),
        grid=(bsz, s_len // tq),
        in_specs=[pl.BlockSpec(memory_space=pltpu.SMEM), main_spec,
                  prev_spec, main_spec, next_spec, prev_spec, main_spec, next_spec,
                  pl.BlockSpec((1, tq, d), lambda b, i: (b, i, 0)), _layer(w_o, layer)],
        out_specs=pl.BlockSpec((1, tq, d), lambda b, i: (b, i, 0)),
        out_shape=jax.ShapeDtypeStruct(x.shape, F32),
        scratch_shapes=[pltpu.VMEM((tq, q_dim), BF16)],
        compiler_params=pltpu.CompilerParams(
            dimension_semantics=("parallel", "parallel"), vmem_limit_bytes=V7X_VMEM_LIMIT),
        name="attn_core",
    )(sink.astype(F32), q, k, k, k, v, v, v, x, w_o)


def _softplus(x):
    return jnp.maximum(x, 0.0) + jnp.log1p(jnp.exp(-jnp.abs(x)))


def _ssd_in_kernel(xp_ref, x_ref, xn_ref, g_ref, w_ref, wdt_ref, cw_ref, cb_ref, dtb_ref,
                   z_out, xbc_out, dt_out, hn_scr, perm_scr, *, tm, fc, d_inner, conv_dim):
    g = g_ref[...]
    _fill_normed(hn_scr, perm_scr, xp_ref, x_ref, xn_ref, g, tm)
    hn_nat = _rms(x_ref[0], g).astype(BF16)
    z_out[0] = _silu(_dot(hn_nat, w_ref[0, :, 0:d_inner])).astype(BF16)
    dt_out[0] = _softplus(_dot(hn_nat, wdt_ref[...]) + dtb_ref[...])
    hn = hn_scr[...]
    for c in range(conv_dim // fc):
        lo, hi = c * fc, (c + 1) * fc
        h = _dot(hn, w_ref[0, :, d_inner + lo:d_inner + hi])
        y = _unpermute(perm_scr, _dwconv(h, cw_ref[:, lo:hi], cb_ref[:, lo:hi], tm), tm)
        xbc_out[0, :, lo:hi] = _silu(y).astype(BF16)


def _pieces(x, n):
    out = []
    for _ in range(n - 1):
        p = x.astype(BF16).astype(F32)
        out.append(p)
        x = x - p
    out.append(x.astype(BF16).astype(F32))
    return out


def _ssd_decay_kernel(dt_ref, a_ref, at_ref, acum_out, tr_out, ew_out, *, heads, cps):
    lc = SSD_CHUNK
    row = lax.broadcasted_iota(jnp.int32, (lc, lc), 0)
    col = lax.broadcasted_iota(jnp.int32, (lc, lc), 1)
    lower = (row >= col).astype(BF16)
    upper = (row <= col).astype(BF16)
    fwd_lane = lax.broadcasted_iota(jnp.int32, (lc, V7X_LANES), 1) < heads
    fwd_row = lax.broadcasted_iota(jnp.int32, (V7X_LANES, lc), 0) < heads
    for k in range(cps):
        rows = slice(k * lc, (k + 1) * lc)
        dt = dt_ref[0, rows, :]
        dtt = dt.T
        ps = [p.astype(BF16) for p in _pieces(dt * a_ref[...], 3)]
        cum_f = (_dot(lower, ps[0]) + _dot(lower, ps[1])) + _dot(lower, ps[2])
        cum_b = (_dot(upper, ps[0]) + _dot(upper, ps[1])) + _dot(upper, ps[2])
        a_cum = jnp.where(fwd_lane, cum_f, cum_b)
        qs = [q.astype(BF16) for q in _pieces(dtt * at_ref[...], 3)]
        cum_tf = (_dot(qs[0], upper) + _dot(qs[1], upper)) + _dot(qs[2], upper)
        cum_tb = (_dot(qs[0], lower) + _dot(qs[1], lower)) + _dot(qs[2], lower)
        a_cum_t = jnp.where(fwd_row, cum_tf, cum_tb)
        a_end = jnp.where(fwd_lane[0:1], a_cum[lc - 1:lc, :], a_cum[0:1, :])
        e1, e2 = _pieces(jnp.exp(a_cum), 2)
        w1, w2 = _pieces(dt * jnp.exp(a_end - a_cum), 2)
        acum_out[0, rows, :] = a_cum
        half = V7X_LANES // 2
        dsum = jnp.log(dtt[0:heads] + dtt[heads:2 * heads])
        tr_out[0, :, rows] = jnp.concatenate(
            [a_cum_t[0:half] - jnp.log(dtt[0:half]), dsum, jnp.zeros((half - heads, lc), F32)], axis=0)
        for d in range(2):
            lanes = slice(d * heads, (d + 1) * heads)
            ew_out[0, rows, d * V7X_LANES:(d + 1) * V7X_LANES] = jnp.concatenate(
                [e1[:, lanes], e2[:, lanes], w1[:, lanes], w2[:, lanes]], axis=1).astype(BF16)


def _ssd_state_step(xbc_ref, ew, state_ref, rows, g, end, d_inner):
    gw = d_inner // SSD_GROUPS
    gn = SSD_GROUPS * D_STATE
    e_exp = ew[:, g * gw:(g + 1) * gw]
    w_exp = ew[:, d_inner + g * gw:d_inner + (g + 1) * gw]
    b_g = xbc_ref[0, rows, d_inner + g * D_STATE:d_inner + (g + 1) * D_STATE]
    c_g = xbc_ref[0, rows, d_inner + gn + g * D_STATE:d_inner + gn + (g + 1) * D_STATE]
    x_g = xbc_ref[0, rows, g * gw:(g + 1) * gw]
    st = state_ref[g]
    y_off = _dot(c_g, st.astype(BF16)) * e_exp
    xw = (x_g.astype(F32) * w_exp).astype(BF16)
    state_ref[g] = e_exp[end:end + 1, :] * st + _dot_tn(b_g, xw)
    return y_off, b_g, c_g, x_g


def _ssd_intra(acum_ref, tr_ref, rows, g, b_g, c_g, x_g, heads, masks):
    from_fwd, from_bwd, lane_head = masks
    hpg = heads // SSD_GROUPS
    cb = _dot_nt(c_g, b_g)
    ms, xb = [], []
    for r in range(hpg):
        h = g * hpg + r
        zf = acum_ref[0, rows, h:h + 1] - tr_ref[0, h:h + 1, rows]
        zb = acum_ref[0, rows, heads + h:heads + h + 1] - tr_ref[0, heads + h:heads + h + 1, rows]
        zd = tr_ref[0, 2 * heads + h:2 * heads + h + 1, rows]
        z = jnp.where(from_fwd, zf, jnp.where(from_bwd, zb, zd))
        ms.append((cb * jnp.exp(z)).astype(BF16))
        xb.append(jnp.where(lane_head == r, x_g, jnp.zeros_like(x_g)))
    return _dot(jnp.concatenate(ms, axis=1), jnp.concatenate(xb, axis=0))


def _intra_masks(lc, gw):
    row = lax.broadcasted_iota(jnp.int32, (lc, lc), 0)
    col = lax.broadcasted_iota(jnp.int32, (lc, lc), 1)
    lane_head = lax.broadcasted_iota(jnp.int32, (lc, gw), 1) // SSD_HEAD_DIM
    return col < row, col > row, lane_head


def _ssd_bwd_kernel(xbc_ref, acum_ref, tr_ref, ew_ref, xsel_ref, y_out, state_ref,
                    *, d_inner, heads, cps, intra_groups):
    @pl.when(pl.program_id(1) == 0)
    def _():
        state_ref[...] = jnp.zeros_like(state_ref)

    lc = SSD_CHUNK
    gw = d_inner // SSD_GROUPS
    masks = _intra_masks(lc, gw)
    for k in range(cps - 1, -1, -1):
        rows = slice(k * lc, (k + 1) * lc)
        ew = _dot(ew_ref[0, rows, V7X_LANES:2 * V7X_LANES], xsel_ref[...])
        for g in range(SSD_GROUPS):
            y, b_g, c_g, x_g = _ssd_state_step(xbc_ref, ew, state_ref, rows, g, 0, d_inner)
            if g in intra_groups:
                y = y + _ssd_intra(acum_ref, tr_ref, rows, g, b_g, c_g, x_g, heads, masks)
            y_out[0, rows, g * gw:(g + 1) * gw] = y


def _ssd_fwd_kernel(xbc_ref, acum_ref, tr_ref, ew_ref, xsel_ref, yb_ref, z_ref, x_ref,
                    dexp_ref, gg_ref, wout_ref, o_ref, state_ref, yn_scr,
                    *, d_inner, heads, cps, intra_groups):
    @pl.when(pl.program_id(1) == 0)
    def _():
        state_ref[...] = jnp.zeros_like(state_ref)

    gw = d_inner // SSD_GROUPS
    lc = SSD_CHUNK
    masks = _intra_masks(lc, gw)
    for k in range(cps):
        rows = slice(k * lc, (k + 1) * lc)
        ew = _dot(ew_ref[0, rows, 0:V7X_LANES], xsel_ref[...])
        for g in range(SSD_GROUPS):
            cols = slice(g * gw, (g + 1) * gw)
            y, b_g, c_g, x_g = _ssd_state_step(xbc_ref, ew, state_ref, rows, g, lc - 1, d_inner)
            if g in intra_groups:
                y = y + _ssd_intra(acum_ref, tr_ref, rows, g, b_g, c_g, x_g, heads, masks)
            y = y + yb_ref[0, rows, cols] + x_g.astype(F32) * dexp_ref[:, cols]
            y = y * z_ref[0, rows, cols].astype(F32)
            y = y * lax.rsqrt(jnp.mean(y * y, axis=-1, keepdims=True) + EPS)
            yn_scr[rows, cols] = (y * gg_ref[:, cols]).astype(BF16)
    o_ref[0] = x_ref[0] + _dot(yn_scr[...], wout_ref[0])


def _ssd_mixer(x, norm_g, w_in, conv_w, conv_b, dt_bias, a_log, d_skip, gate_g, w_out, layer,
               *, tm=SSD_IN_TILE, fc=SSD_IN_COLS, cps=SSD_CHUNKS_PER_STEP):
    bsz, s_len, d = x.shape
    d_inner = w_out.shape[1]
    heads = d_skip.shape[0]
    conv_dim = conv_w.shape[1]
    tm = min(tm, s_len)
    dt_pad = V7X_LANES - 2 * heads
    w_dt = jnp.pad(w_in[layer, :, d_inner + conv_dim:], ((0, 0), (0, dt_pad)))
    dt_b = jnp.pad(dt_bias.reshape(1, 2 * heads).astype(F32), ((0, 0), (0, dt_pad)))
    tile = lambda w: pl.BlockSpec((1, tm, w), lambda b, i: (b, i, 0))
    sz, xbc, dt = pl.pallas_call(
        functools.partial(_ssd_in_kernel, tm=tm, fc=fc, d_inner=d_inner, conv_dim=conv_dim),
        grid=(bsz, s_len // tm),
        in_specs=_halo_specs(tm, s_len, d) + [
            _full((1, d)), _layer(w_in, layer), _full((d, V7X_LANES)), _full(conv_w.shape),
            _full((1, conv_dim)), _full((1, V7X_LANES))],
        out_specs=[tile(d_inner), tile(conv_dim), tile(V7X_LANES)],
        out_shape=[jax.ShapeDtypeStruct((bsz, s_len, d_inner), BF16),
                   jax.ShapeDtypeStruct((bsz, s_len, conv_dim), BF16),
                   jax.ShapeDtypeStruct((bsz, s_len, V7X_LANES), F32)],
        scratch_shapes=[pltpu.VMEM((tm + 2 * F32_SUBLANES, d), BF16),
                        pltpu.VMEM((d // V7X_LANES, tm, V7X_LANES), F32)],
        compiler_params=pltpu.CompilerParams(
            dimension_semantics=("parallel", "parallel"), vmem_limit_bytes=V7X_VMEM_LIMIT),
        name="ssd_in",
    )(x, x, x, norm_g.reshape(1, d), w_in, w_dt, conv_w, conv_b.reshape(1, conv_dim), dt_b)

    lc = SSD_CHUNK
    cps = min(cps, s_len // lc)
    assert s_len % tm == 0 and tm % (F32_SUBLANES * F32_SUBLANES) == 0 and conv_dim % fc == 0
    assert s_len % (cps * lc) == 0 and heads % SSD_GROUPS == 0 and 2 * heads <= V7X_LANES // 2
    assert d_inner == heads * SSD_HEAD_DIM and conv_dim == d_inner + 2 * SSD_GROUPS * D_STATE
    nc = s_len // (lc * cps)
    a_lanes = jnp.pad((-jnp.exp(a_log.astype(F32))).reshape(1, 2 * heads), ((0, 0), (0, dt_pad)))
    step = lambda w: pl.BlockSpec((1, cps * lc, w), lambda b, c: (b, c, 0))
    acum, tr, ew = pl.pallas_call(
        functools.partial(_ssd_decay_kernel, heads=heads, cps=cps),
        grid=(bsz, nc),
        in_specs=[step(V7X_LANES), _full((1, V7X_LANES)), _full((V7X_LANES, 1))],
        out_specs=[step(V7X_LANES), pl.BlockSpec((1, V7X_LANES, cps * lc), lambda b, c: (b, 0, c)),
                   step(2 * V7X_LANES)],
        out_shape=[jax.ShapeDtypeStruct((bsz, s_len, V7X_LANES), F32),
                   jax.ShapeDtypeStruct((bsz, V7X_LANES, s_len), F32),
                   jax.ShapeDtypeStruct((bsz, s_len, 2 * V7X_LANES), BF16)],
        compiler_params=pltpu.CompilerParams(
            dimension_semantics=("parallel", "parallel"), vmem_limit_bytes=V7X_VMEM_LIMIT),
        name="ssd_decay",
    )(dt, a_lanes, a_lanes.reshape(V7X_LANES, 1))

    xr = lax.broadcasted_iota(jnp.int32, (4 * heads, 2 * d_inner), 0)
    xc = lax.broadcasted_iota(jnp.int32, (4 * heads, 2 * d_inner), 1)
    xsel = ((xr % heads == (xc % d_inner) // SSD_HEAD_DIM)
            & (xr // (2 * heads) == xc // d_inner)).astype(BF16)
    bwd_share = SSD_BWD_INTRA_GROUPS
    state = pltpu.VMEM((SSD_GROUPS, D_STATE, d_inner // SSD_GROUPS), F32)
    rev = lambda w: pl.BlockSpec((1, cps * lc, w), lambda b, c: (b, nc - 1 - c, 0))
    fwd = step
    y_b = pl.pallas_call(
        functools.partial(_ssd_bwd_kernel, d_inner=d_inner, heads=heads, cps=cps,
                          intra_groups=range(0, bwd_share)),
        grid=(bsz, nc),
        in_specs=[rev(conv_dim), rev(V7X_LANES),
                  pl.BlockSpec((1, V7X_LANES, cps * lc), lambda b, c: (b, 0, nc - 1 - c)),
                  rev(2 * V7X_LANES), _full(xsel.shape)],
        out_specs=rev(d_inner),
        out_shape=jax.ShapeDtypeStruct((bsz, s_len, d_inner), F32),
        scratch_shapes=[state],
        compiler_params=pltpu.CompilerParams(
            dimension_semantics=("parallel", "arbitrary"), vmem_limit_bytes=V7X_VMEM_LIMIT),
        name="ssd_scan_bwd",
    )(xbc, acum, tr, ew, xsel)

    return pl.pallas_call(
        functools.partial(_ssd_fwd_kernel, d_inner=d_inner, heads=heads, cps=cps,
                          intra_groups=range(bwd_share, SSD_GROUPS)),
        grid=(bsz, nc),
        in_specs=[fwd(conv_dim), fwd(V7X_LANES),
                  pl.BlockSpec((1, V7X_LANES, cps * lc), lambda b, c: (b, 0, c)),
                  fwd(2 * V7X_LANES), _full(xsel.shape), fwd(d_inner), fwd(d_inner), fwd(d),
                  _full((1, d_inner)), _full((1, d_inner)), _layer(w_out, layer)],
        out_specs=fwd(d),
        out_shape=jax.ShapeDtypeStruct(x.shape, F32),
        scratch_shapes=[state, pltpu.VMEM((cps * lc, d_inner), BF16)],
        compiler_params=pltpu.CompilerParams(
            dimension_semantics=("parallel", "arbitrary"), vmem_limit_bytes=V7X_VMEM_LIMIT),
        name="ssd_scan_fwd",
    )(xbc, acum, tr, ew, xsel, y_b, sz, x,
      jnp.broadcast_to(d_skip.astype(F32)[:, None], (heads, SSD_HEAD_DIM)).reshape(1, d_inner),
      gate_g.reshape(1, d_inner).astype(F32), w_out)


def kernel(x, attn_norm, attn_w_qkv, attn_q_norm, attn_k_norm, attn_sink, attn_w_o, ssd_norm, ssd_w_in, ssd_conv_w, ssd_conv_b, ssd_dt_bias, ssd_a_log, ssd_d, ssd_gate_norm, ssd_w_out, ffn_norm, ffn_w_up, ffn_conv_w, ffn_conv_b, ffn_w_down):
    depth = ffn_norm.shape[0]
    rope = _rope_tables(x.shape[1])
    w_qkv, w_o = attn_w_qkv.astype(BF16), attn_w_o.astype(BF16)
    w_in, w_out = ssd_w_in.astype(BF16), ssd_w_out.astype(BF16)
    w_up, w_down = ffn_w_up.astype(BF16), ffn_w_down.astype(BF16)
    for i in range(depth):
        j = i // 2
        if i % 2 == 0:
            x = _window_attention(x, attn_norm[j], w_qkv, attn_q_norm[j], attn_k_norm[j],
                                  attn_sink[j], w_o, j, rope)
        else:
            x = _ssd_mixer(x, ssd_norm[j], w_in, ssd_conv_w[j], ssd_conv_b[j], ssd_dt_bias[j],
                           ssd_a_log[j], ssd_d[j], ssd_gate_norm[j], w_out, j)
        x = _conv_ffn(x, ffn_norm[i], w_up, ffn_conv_w[i], ffn_conv_b[i], w_down, i)
    return x
```

```python
import functools

import jax
import jax.numpy as jnp
from jax import lax
from jax.experimental import pallas as pl
from jax.experimental.pallas import tpu as pltpu

F32 = jnp.float32
BF16 = jnp.bfloat16

EPS = 1e-6
HEAD_DIM = 64
ROT_DIM = HEAD_DIM // 4
ROPE_THETA = 500000.0
ATTN_BLOCK = 128
SSD_HEAD_DIM = 64
SSD_GROUPS = 8
D_STATE = 128
SSD_CHUNK = 128

V7X_LANES = 128
F32_SUBLANES = 8
V7X_VMEM_BYTES = 64 * 1024 * 1024
V7X_VMEM_LIMIT = V7X_VMEM_BYTES * 7 // 8
NORM_HEAD_SLOTS = 16

FFN_TILE, FFN_COLS = 1024, 256
QKV_TILE, ATTN_TILE = 1024, 512
SSD_IN_TILE, SSD_IN_COLS = 512, 512
SSD_CHUNKS_PER_STEP = 4
SSD_BWD_INTRA_GROUPS = 4


def _rms(x, g):
    return x * lax.rsqrt(jnp.mean(x * x, axis=-1, keepdims=True) + EPS) * g


def _silu(x):
    h = 0.5 * x
    return h + h * jnp.tanh(h)


def _dot(a, b):
    return jnp.dot(a, b, preferred_element_type=F32)


def _dot_nt(a, b):
    return lax.dot_general(a, b, (((1,), (1,)), ((), ())), preferred_element_type=F32)


def _dot_tn(a, b):
    return lax.dot_general(a, b, (((0,), (0,)), ((), ())), preferred_element_type=F32)


def _halo_specs(tm, s_len, d):
    r = tm // F32_SUBLANES
    last = s_len // F32_SUBLANES - 1
    return [
        pl.BlockSpec((1, F32_SUBLANES, d), lambda b, i: (b, jnp.maximum(i * r - 1, 0), 0)),
        pl.BlockSpec((1, tm, d), lambda b, i: (b, i, 0)),
        pl.BlockSpec((1, F32_SUBLANES, d), lambda b, i: (b, jnp.minimum((i + 1) * r, last), 0)),
    ]


def _full(shape):
    return pl.BlockSpec(shape, lambda b, i: (0,) * len(shape), pipeline_mode=pl.Buffered(1))


def _layer(stacked, j):
    return pl.BlockSpec((1,) + stacked.shape[1:], lambda b, i: (j, 0, 0), pipeline_mode=pl.Buffered(1))


def _perm_base(a, tm):
    p = tm // F32_SUBLANES
    t = a * F32_SUBLANES
    return (t % p) * F32_SUBLANES + t // p


def _fill_normed(hn_scr, perm_scr, xp_ref, x_ref, xn_ref, g, tm):
    i = pl.program_id(1)
    n = pl.num_programs(1)
    h = F32_SUBLANES
    keep_p = (i > 0).astype(F32)
    keep_n = (i < n - 1).astype(F32)
    halo = jnp.concatenate([_rms(xp_ref[0], g) * keep_p, _rms(xn_ref[0], g) * keep_n], axis=0)
    hn_scr[0:2 * h, :] = halo.astype(BF16)
    xn = _rms(x_ref[0], g)
    slabs = xn.shape[1] // V7X_LANES
    for a in range(tm // h):
        for j in range(slabs):
            perm_scr[j, pl.ds(_perm_base(a, tm), h, stride=h), :] = (
                xn[a * h:(a + 1) * h, j * V7X_LANES:(j + 1) * V7X_LANES])
    for j in range(slabs):
        hn_scr[2 * h:2 * h + tm, j * V7X_LANES:(j + 1) * V7X_LANES] = perm_scr[j].astype(BF16)


def _unpermute(perm_scr, y, tm):
    h = F32_SUBLANES
    slabs = y.shape[1] // V7X_LANES
    for j in range(slabs):
        perm_scr[j] = y[:, j * V7X_LANES:(j + 1) * V7X_LANES]
    rows = []
    for a in range(tm // h):
        rows.append(jnp.concatenate(
            [perm_scr[j, pl.ds(_perm_base(a, tm), h, stride=h), :] for j in range(slabs)], axis=1))
    return jnp.concatenate(rows, axis=0)


def _dwconv(hh, w, b, tm):
    h = F32_SUBLANES
    k_w = w.shape[0]
    pad = k_w // 2
    hp, hx, hm = hh[0:h], hh[h:2 * h], hh[2 * h:]
    sub = lax.broadcasted_iota(jnp.int32, hp.shape, 0)
    before = []
    for e in range(pad, 0, -1):
        src = jnp.where(sub == h - 1, pltpu.roll(hp, e - 1, 0) if e > 1 else hp, hm[tm - e * h:tm - (e - 1) * h])
        before.append(pltpu.roll(src, 1, 0))
    after = []
    for e in range(pad):
        src = jnp.where(sub == 0, pltpu.roll(hx, h - e, 0) if e > 0 else hx, hm[e * h:(e + 1) * h])
        after.append(pltpu.roll(src, h - 1, 0))
    ext = jnp.concatenate(before + [hm] + after, axis=0)
    y = b + ext[0:tm] * w[0:1]
    for k in range(1, k_w):
        y = y + ext[k * h:k * h + tm] * w[k:k + 1]
    return y


def _ffn_kernel(xp_ref, x_ref, xn_ref, g_ref, wup_ref, cw_ref, cb_ref, wdn_ref, o_ref, hn_scr,
                perm_scr, act_scr, *, tm, fc, d_ff):
    _fill_normed(hn_scr, perm_scr, xp_ref, x_ref, xn_ref, g_ref[...], tm)
    hn = hn_scr[...]
    for c in range(d_ff // fc):
        lo, hi = c * fc, (c + 1) * fc
        hg = _dot(hn, wup_ref[0, :, lo:hi])
        hv = _dot(hn, wup_ref[0, :, d_ff + lo:d_ff + hi])
        gate = _dwconv(hg, cw_ref[:, lo:hi], cb_ref[:, lo:hi], tm)
        val = _dwconv(hv, cw_ref[:, d_ff + lo:d_ff + hi], cb_ref[:, d_ff + lo:d_ff + hi], tm)
        act_scr[:, lo:hi] = (_silu(gate) * val).astype(BF16)
    o_ref[0] = x_ref[0] + _unpermute(perm_scr, _dot(act_scr[...], wdn_ref[0]), tm)


def _conv_ffn(x, g, w_up, conv_w, conv_b, w_down, layer, *, tm=FFN_TILE, fc=FFN_COLS):
    bsz, s_len, d = x.shape
    d_ff = w_down.shape[1]
    tm = min(tm, s_len)
    assert s_len % tm == 0 and tm % (F32_SUBLANES * F32_SUBLANES) == 0 and d_ff % fc == 0
    kern = functools.partial(_ffn_kernel, tm=tm, fc=fc, d_ff=d_ff)
    return pl.pallas_call(
        kern,
        grid=(bsz, s_len // tm),
        in_specs=_halo_specs(tm, s_len, d) + [
            _full((1, d)), _layer(w_up, layer), _full(conv_w.shape), _full((1, 2 * d_ff)),
            _layer(w_down, layer)],
        out_specs=pl.BlockSpec((1, tm, d), lambda b, i: (b, i, 0)),
        out_shape=jax.ShapeDtypeStruct(x.shape, F32),
        scratch_shapes=[pltpu.VMEM((tm + 2 * F32_SUBLANES, d), BF16),
                        pltpu.VMEM((d // V7X_LANES, tm, V7X_LANES), F32),
                        pltpu.VMEM((tm, d_ff), BF16)],
        compiler_params=pltpu.CompilerParams(
            dimension_semantics=("parallel", "parallel"), vmem_limit_bytes=V7X_VMEM_LIMIT),
        name="conv_ffn",
    )(x, x, x, g.reshape(1, d), w_up, conv_w, conv_b.reshape(1, -1), w_down)


def _rope_tables(s_len):
    half = ROT_DIM // 2
    pos = jnp.arange(s_len, dtype=F32)
    inv_freq = ROPE_THETA ** (-(jnp.arange(0, ROT_DIM, 2, dtype=F32) / ROT_DIM))
    ang = pos[:, None] * inv_freq[None, :]
    cos, sin = jnp.cos(ang), jnp.sin(ang)
    rest = HEAD_DIM - ROT_DIM
    c = jnp.concatenate([cos, cos, jnp.ones((s_len, rest), F32)], axis=1)
    s1 = jnp.concatenate([-sin, jnp.zeros((s_len, half + rest), F32)], axis=1)
    s2 = jnp.concatenate([jnp.zeros((s_len, half), F32), sin, jnp.zeros((s_len, rest), F32)], axis=1)
    rep = V7X_LANES // HEAD_DIM
    return jnp.tile(c, (1, rep)), jnp.tile(s1, (1, rep)), jnp.tile(s2, (1, rep))


def _norm_rope(t, red, expd, g, c, s1, s2, scale):
    half = ROT_DIM // 2
    rs = lax.rsqrt(_dot((t * t).astype(BF16), red) + EPS)
    hi = rs.astype(BF16).astype(F32)
    lane = lax.broadcasted_iota(jnp.int32, rs.shape, 1)
    packed = jnp.where(lane < NORM_HEAD_SLOTS, hi, pltpu.roll(rs - hi, NORM_HEAD_SLOTS, 1))
    tn = t * _dot(packed.astype(BF16), expd) * g
    outs = []
    for j in range(t.shape[1] // V7X_LANES):
        tc = tn[:, j * V7X_LANES:(j + 1) * V7X_LANES]
        tr = tc * c + pltpu.roll(tc, V7X_LANES - half, 1) * s1 + pltpu.roll(tc, half, 1) * s2
        outs.append(tr * scale if scale != 1.0 else tr)
    return jnp.concatenate(outs, axis=1)


def _replicate_heads(t, group):
    lane = lax.broadcasted_iota(jnp.int32, (t.shape[0], V7X_LANES), 1)
    first = lane < HEAD_DIM
    outs = []
    for j in range(t.shape[1] // V7X_LANES):
        tc = t[:, j * V7X_LANES:(j + 1) * V7X_LANES]
        tc_sw = pltpu.roll(tc, HEAD_DIM, 1)
        even = jnp.where(first, tc, tc_sw)
        odd = jnp.where(first, tc_sw, tc)
        outs += [even] * (group // 2) + [odd] * (group // 2)
    return jnp.concatenate(outs, axis=1)


def _qkv_kernel(x_ref, g_ref, w_ref, qg_ref, kg_ref, redq_ref, expq_ref, redk_ref, expk_ref,
                c_ref, s1_ref, s2_ref, q_out, k_out, v_out, *, q_dim, kv_dim, group):
    hn = _rms(x_ref[0], g_ref[...]).astype(BF16)
    c, s1, s2 = c_ref[...], s1_ref[...], s2_ref[...]
    q = _dot(hn, w_ref[0, :, 0:q_dim])
    q_out[0] = _norm_rope(q, redq_ref[...], expq_ref[...], qg_ref[...], c, s1, s2,
                          HEAD_DIM ** -0.5).astype(BF16)
    k = _dot(hn, w_ref[0, :, q_dim:q_dim + kv_dim])
    kr = _norm_rope(k, redk_ref[...], expk_ref[...], kg_ref[...], c, s1, s2, 1.0)
    k_out[0] = _replicate_heads(kr, group).astype(BF16)
    v = _dot(hn, w_ref[0, :, q_dim + kv_dim:q_dim + 2 * kv_dim])
    v_out[0] = _replicate_heads(v, group).astype(BF16)


def _attn_kernel(sink_ref, q_ref, kp_ref, k_ref, kn_ref, vp_ref, v_ref, vn_ref, x_ref, wo_ref,
                 o_ref, o_scr, *, tq, n_kv, group):
    i = pl.program_id(1)
    n = pl.num_programs(1)
    blk = ATTN_BLOCK
    gw = group * HEAD_DIM
    k_all = jnp.concatenate([kp_ref[0], k_ref[0], kn_ref[0]], axis=0)
    v_all = jnp.concatenate([vp_ref[0], v_ref[0], vn_ref[0]], axis=0)
    row = lax.broadcasted_iota(jnp.int32, (group * blk, 3 * blk), 0) % blk
    col = lax.broadcasted_iota(jnp.int32, (group * blk, 3 * blk), 1)
    band = (col >= row) & (col <= row + 2 * blk)
    head_of_lane = lax.broadcasted_iota(jnp.int32, (blk, gw), 1) // HEAD_DIM
    head_of_row = lax.broadcasted_iota(jnp.int32, (group * blk, 1), 0) // blk
    nblk = tq // blk
    for jb in range(nblk):
        valid = band
        if jb == 0:
            valid = valid & ((col >= blk) | (i > 0))
        if jb == nblk - 1:
            valid = valid & ((col < 2 * blk) | (i < n - 1))
        for kh in range(n_kv):
            qg = q_ref[0, jb * blk:(jb + 1) * blk, kh * gw:(kh + 1) * gw]
            kw = k_all[jb * blk:(jb + 3) * blk, kh * gw:(kh + 1) * gw]
            vw = v_all[jb * blk:(jb + 3) * blk, kh * gw:(kh + 1) * gw]
            qs = jnp.concatenate(
                [jnp.where(head_of_lane == hl, qg, jnp.zeros_like(qg)) for hl in range(group)], axis=0)
            sink = jnp.full((group * blk, 1), sink_ref[kh * group + group - 1], F32)
            for hl in range(group - 2, -1, -1):
                sink = jnp.where(head_of_row == hl, sink_ref[kh * group + hl], sink)
            s = jnp.where(valid, _dot_nt(qs, kw), -1e30)
            m = jnp.maximum(jnp.max(s, axis=-1, keepdims=True), sink)
            p = jnp.exp(s - m)
            denom = jnp.sum(p, axis=-1, keepdims=True) + jnp.exp(sink - m)
            o = _dot(p.astype(BF16), vw) / denom
            og = o[(group - 1) * blk:group * blk]
            for hl in range(group - 2, -1, -1):
                og = jnp.where(head_of_lane == hl, o[hl * blk:(hl + 1) * blk], og)
            o_scr[jb * blk:(jb + 1) * blk, kh * gw:(kh + 1) * gw] = og.astype(BF16)
    o_ref[0] = x_ref[0] + _dot(o_scr[...], wo_ref[0])


def _window_attention(x, norm_g, w_qkv, q_g, k_g, sink, w_o, layer, rope,
                      *, tm=QKV_TILE, tq=ATTN_TILE):
    bsz, s_len, d = x.shape
    n_heads = sink.shape[0]
    q_dim = w_o.shape[1]
    kv_dim = (w_qkv.shape[2] - q_dim) // 2
    n_kv = kv_dim // HEAD_DIM
    group = n_heads // n_kv
    tm = min(tm, s_len)
    tq = min(tq, s_len)
    assert s_len % tm == 0 and s_len % tq == 0 and tq % ATTN_BLOCK == 0
    assert n_heads <= NORM_HEAD_SLOTS and group % 2 == 0 and q_dim == n_heads * HEAD_DIM
    c, s1, s2 = rope

    def reduce_expand(width):
        i = lax.broadcasted_iota(jnp.int32, (width, V7X_LANES), 0)
        j = lax.broadcasted_iota(jnp.int32, (width, V7X_LANES), 1)
        red = jnp.where(i // HEAD_DIM == j, 1.0 / HEAD_DIM, 0.0).astype(BF16)
        expd = ((j.T % NORM_HEAD_SLOTS == i.T // HEAD_DIM) & (j.T < 2 * NORM_HEAD_SLOTS)).astype(BF16)
        return red, expd

    redq, expq = reduce_expand(q_dim)
    redk, expk = reduce_expand(kv_dim)
    tile = lambda w: pl.BlockSpec((1, tm, w), lambda b, i: (b, i, 0))
    rope_spec = pl.BlockSpec((tm, V7X_LANES), lambda b, i: (i, 0))
    q, k, v = pl.pallas_call(
        functools.partial(_qkv_kernel, q_dim=q_dim, kv_dim=kv_dim, group=group),
        grid=(bsz, s_len // tm),
        in_specs=[tile(d), _full((1, d)), _layer(w_qkv, layer), _full((1, q_dim)),
                  _full((1, kv_dim)), _full(redq.shape), _full(expq.shape), _full(redk.shape),
                  _full(expk.shape), rope_spec, rope_spec, rope_spec],
        out_specs=[tile(q_dim), tile(q_dim), tile(q_dim)],
        out_shape=[jax.ShapeDtypeStruct((bsz, s_len, q_dim), BF16)] * 3,
        compiler_params=pltpu.CompilerParams(
            dimension_semantics=("parallel", "parallel"), vmem_limit_bytes=V7X_VMEM_LIMIT),
        name="attn_qkv",
    )(x, norm_g.reshape(1, d), w_qkv, jnp.tile(q_g, n_heads).reshape(1, q_dim),
      jnp.tile(k_g, n_kv).reshape(1, kv_dim), redq, expq, redk, expk, c, s1, s2)

    r = tq // ATTN_BLOCK
    last = s_len // ATTN_BLOCK - 1
    prev_spec = pl.BlockSpec((1, ATTN_BLOCK, q_dim), lambda b, i: (b, jnp.maximum(i * r - 1, 0), 0))
    main_spec = pl.BlockSpec((1, tq, q_dim), lambda b, i: (b, i, 0))
    next_spec = pl.BlockSpec((1, ATTN_BLOCK, q_dim), lambda b, i: (b, jnp.minimum((i + 1) * r, last), 0))
    return pl.pallas_call(
        functools.partial(_attn_kernel, tq=tq, n_kv=n_kv, group=group),
        grid=(bsz, s_len // tq),
        in_specs=[pl.BlockSpec(memory_space=pltpu.SMEM), main_spec,
                  prev_spec, main_spec, next_spec, prev_spec, main_spec, next_spec,
                  pl.BlockSpec((1, tq, d), lambda b, i: (b, i, 0)), _layer(w_o, layer)],
        out_specs=pl.BlockSpec((1, tq, d), lambda b, i: (b, i, 0)),
        out_shape=jax.ShapeDtypeStruct(x.shape, F32),
        scratch_shapes=[pltpu.VMEM((tq, q_dim), BF16)],
        compiler_params=pltpu.CompilerParams(
            dimension_semantics=("parallel", "parallel"), vmem_limit_bytes=V7X_VMEM_LIMIT),
        name="attn_core",
    )(sink.astype(F32), q, k, k, k, v, v, v, x, w_o)


def _softplus(x):
    return jnp.maximum(x, 0.0) + jnp.log1p(jnp.exp(-jnp.abs(x)))


def _ssd_in_kernel(xp_ref, x_ref, xn_ref, g_ref, w_ref, wdt_ref, cw_ref, cb_ref, dtb_ref, a_ref, at_ref,
                   z_out, xbc_out, acum_out, tr_out, ew_out, hn_scr, perm_scr, dt_scr,
                   *, tm, fc, d_inner, conv_dim, heads):
    g = g_ref[...]
    _fill_normed(hn_scr, perm_scr, xp_ref, x_ref, xn_ref, g, tm)
    hn_nat = _rms(x_ref[0], g).astype(BF16)
    z_out[0] = _silu(_dot(hn_nat, w_ref[0, :, 0:d_inner])).astype(BF16)
    dt_scr[0] = _softplus(_dot(hn_nat, wdt_ref[...]) + dtb_ref[...])
    _ssd_decay_terms(dt_scr, a_ref, at_ref, acum_out, tr_out, ew_out, heads, tm // SSD_CHUNK)
    hn = hn_scr[...]
    for c in range(conv_dim // fc):
        lo, hi = c * fc, (c + 1) * fc
        h = _dot(hn, w_ref[0, :, d_inner + lo:d_inner + hi])
        y = _unpermute(perm_scr, _dwconv(h, cw_ref[:, lo:hi], cb_ref[:, lo:hi], tm), tm)
        xbc_out[0, :, lo:hi] = _silu(y).astype(BF16)


def _pieces(x, n):
    out = []
    for _ in range(n - 1):
        p = x.astype(BF16).astype(F32)
        out.append(p)
        x = x - p
    out.append(x.astype(BF16).astype(F32))
    return out


def _ssd_decay_terms(dt_ref, a_ref, at_ref, acum_out, tr_out, ew_out, heads, n_chunks):
    lc = SSD_CHUNK
    row = lax.broadcasted_iota(jnp.int32, (lc, lc), 0)
    col = lax.broadcasted_iota(jnp.int32, (lc, lc), 1)
    lower = (row >= col).astype(BF16)
    upper = (row <= col).astype(BF16)
    fwd_lane = lax.broadcasted_iota(jnp.int32, (lc, V7X_LANES), 1) < heads
    fwd_row = lax.broadcasted_iota(jnp.int32, (V7X_LANES, lc), 0) < heads
    for k in range(n_chunks):
        rows = slice(k * lc, (k + 1) * lc)
        dt = dt_ref[0, rows, :]
        dtt = dt.T
        ps = [p.astype(BF16) for p in _pieces(dt * a_ref[...], 3)]
        cum_f = (_dot(lower, ps[0]) + _dot(lower, ps[1])) + _dot(lower, ps[2])
        cum_b = (_dot(upper, ps[0]) + _dot(upper, ps[1])) + _dot(upper, ps[2])
        a_cum = jnp.where(fwd_lane, cum_f, cum_b)
        qs = [q.astype(BF16) for q in _pieces(dtt * at_ref[...], 3)]
        cum_tf = (_dot(qs[0], upper) + _dot(qs[1], upper)) + _dot(qs[2], upper)
        cum_tb = (_dot(qs[0], lower) + _dot(qs[1], lower)) + _dot(qs[2], lower)
        a_cum_t = jnp.where(fwd_row, cum_tf, cum_tb)
        a_end = jnp.where(fwd_lane[0:1], a_cum[lc - 1:lc, :], a_cum[0:1, :])
        e1, e2 = _pieces(jnp.exp(a_cum), 2)
        w1, w2 = _pieces(dt * jnp.exp(a_end - a_cum), 2)
        acum_out[0, rows, :] = a_cum
        half = V7X_LANES // 2
        dsum = jnp.log(dtt[0:heads] + dtt[heads:2 * heads])
        tr_out[0, :, rows] = jnp.concatenate(
            [a_cum_t[0:half] - jnp.log(dtt[0:half]), dsum, jnp.zeros((half - heads, lc), F32)], axis=0)
        for d in range(2):
            lanes = slice(d * heads, (d + 1) * heads)
            ew_out[0, rows, d * V7X_LANES:(d + 1) * V7X_LANES] = jnp.concatenate(
                [e1[:, lanes], e2[:, lanes], w1[:, lanes], w2[:, lanes]], axis=1).astype(BF16)


def _ssd_state_step(xbc_ref, ew, state_ref, rows, g, end, d_inner):
    gw = d_inner // SSD_GROUPS
    gn = SSD_GROUPS * D_STATE
    e_exp = ew[:, g * gw:(g + 1) * gw]
    w_exp = ew[:, d_inner + g * gw:d_inner + (g + 1) * gw]
    b_g = xbc_ref[0, rows, d_inner + g * D_STATE:d_inner + (g + 1) * D_STATE]
    c_g = xbc_ref[0, rows, d_inner + gn + g * D_STATE:d_inner + gn + (g + 1) * D_STATE]
    x_g = xbc_ref[0, rows, g * gw:(g + 1) * gw]
    st = state_ref[g]
    y_off = _dot(c_g, st.astype(BF16)) * e_exp
    xw = (x_g.astype(F32) * w_exp).astype(BF16)
    state_ref[g] = e_exp[end:end + 1, :] * st + _dot_tn(b_g, xw)
    return y_off, b_g, c_g, x_g


def _ssd_intra(acum_ref, tr_ref, rows, g, b_g, c_g, x_g, heads, masks):
    from_fwd, from_bwd, lane_head = masks
    hpg = heads // SSD_GROUPS
    cb = _dot_nt(c_g, b_g)
    ms, xb = [], []
    for r in range(hpg):
        h = g * hpg + r
        zf = acum_ref[0, rows, h:h + 1] - tr_ref[0, h:h + 1, rows]
        zb = acum_ref[0, rows, heads + h:heads + h + 1] - tr_ref[0, heads + h:heads + h + 1, rows]
        zd = tr_ref[0, 2 * heads + h:2 * heads + h + 1, rows]
        z = jnp.where(from_fwd, zf, jnp.where(from_bwd, zb, zd))
        ms.append((cb * jnp.exp(z)).astype(BF16))
        xb.append(jnp.where(lane_head == r, x_g, jnp.zeros_like(x_g)))
    return _dot(jnp.concatenate(ms, axis=1), jnp.concatenate(xb, axis=0))


def _intra_masks(lc, gw):
    row = lax.broadcasted_iota(jnp.int32, (lc, lc), 0)
    col = lax.broadcasted_iota(jnp.int32, (lc, lc), 1)
    lane_head = lax.broadcasted_iota(jnp.int32, (lc, gw), 1) // SSD_HEAD_DIM
    return col < row, col > row, lane_head


def _ssd_bwd_kernel(xbc_ref, acum_ref, tr_ref, ew_ref, xsel_ref, y_out, state_ref,
                    *, d_inner, heads, cps, intra_groups):
    @pl.when(pl.program_id(1) == 0)
    def _():
        state_ref[...] = jnp.zeros_like(state_ref)

    lc = SSD_CHUNK
    gw = d_inner // SSD_GROUPS
    masks = _intra_masks(lc, gw)
    for k in range(cps - 1, -1, -1):
        rows = slice(k * lc, (k + 1) * lc)
        ew = _dot(ew_ref[0, rows, V7X_LANES:2 * V7X_LANES], xsel_ref[...])
        for g in range(SSD_GROUPS):
            y, b_g, c_g, x_g = _ssd_state_step(xbc_ref, ew, state_ref, rows, g, 0, d_inner)
            if g in intra_groups:
                y = y + _ssd_intra(acum_ref, tr_ref, rows, g, b_g, c_g, x_g, heads, masks)
            y_out[0, rows, g * gw:(g + 1) * gw] = y


def _ssd_fwd_kernel(xbc_ref, acum_ref, tr_ref, ew_ref, xsel_ref, yb_ref, z_ref, x_ref,
                    dexp_ref, gg_ref, wout_ref, o_ref, state_ref, yn_scr,
                    *, d_inner, heads, cps, intra_groups):
    @pl.when(pl.program_id(1) == 0)
    def _():
        state_ref[...] = jnp.zeros_like(state_ref)

    gw = d_inner // SSD_GROUPS
    lc = SSD_CHUNK
    masks = _intra_masks(lc, gw)
    for k in range(cps):
        rows = slice(k * lc, (k + 1) * lc)
        ew = _dot(ew_ref[0, rows, 0:V7X_LANES], xsel_ref[...])
        for g in range(SSD_GROUPS):
            cols = slice(g * gw, (g + 1) * gw)
            y, b_g, c_g, x_g = _ssd_state_step(xbc_ref, ew, state_ref, rows, g, lc - 1, d_inner)
            if g in intra_groups:
                y = y + _ssd_intra(acum_ref, tr_ref, rows, g, b_g, c_g, x_g, heads, masks)
            y = y + yb_ref[0, rows, cols] + x_g.astype(F32) * dexp_ref[:, cols]
            y = y * z_ref[0, rows, cols].astype(F32)
            y = y * lax.rsqrt(jnp.mean(y * y, axis=-1, keepdims=True) + EPS)
            yn_scr[rows, cols] = (y * gg_ref[:, cols]).astype(BF16)
    o_ref[0] = x_ref[0] + _dot(yn_scr[...], wout_ref[0])


def _ssd_mixer(x, norm_g, w_in, w_dt, conv_w, conv_b, dt_bias, a_log, d_skip, gate_g, w_out, layer,
               *, tm=SSD_IN_TILE, fc=SSD_IN_COLS, cps=SSD_CHUNKS_PER_STEP):
    bsz, s_len, d = x.shape
    d_inner = w_out.shape[1]
    heads = d_skip.shape[0]
    conv_dim = conv_w.shape[1]
    tm = min(tm, s_len)
    dt_pad = V7X_LANES - 2 * heads
    w_dt = jnp.pad(w_dt, ((0, 0), (0, dt_pad)))
    dt_b = jnp.pad(dt_bias.reshape(1, 2 * heads).astype(F32), ((0, 0), (0, dt_pad)))
    a_lanes = jnp.pad((-jnp.exp(a_log.astype(F32))).reshape(1, 2 * heads), ((0, 0), (0, dt_pad)))
    tile = lambda w: pl.BlockSpec((1, tm, w), lambda b, i: (b, i, 0))
    sz, xbc, acum, tr, ew = pl.pallas_call(
        functools.partial(_ssd_in_kernel, tm=tm, fc=fc, d_inner=d_inner, conv_dim=conv_dim, heads=heads),
        grid=(bsz, s_len // tm),
        in_specs=_halo_specs(tm, s_len, d) + [
            _full((1, d)), _layer(w_in, layer), _full((d, V7X_LANES)), _full(conv_w.shape),
            _full((1, conv_dim)), _full((1, V7X_LANES)), _full((1, V7X_LANES)), _full((V7X_LANES, 1))],
        out_specs=[tile(d_inner), tile(conv_dim), tile(V7X_LANES),
                   pl.BlockSpec((1, V7X_LANES, tm), lambda b, i: (b, 0, i)), tile(2 * V7X_LANES)],
        out_shape=[jax.ShapeDtypeStruct((bsz, s_len, d_inner), BF16),
                   jax.ShapeDtypeStruct((bsz, s_len, conv_dim), BF16),
                   jax.ShapeDtypeStruct((bsz, s_len, V7X_LANES), F32),
                   jax.ShapeDtypeStruct((bsz, V7X_LANES, s_len), F32),
                   jax.ShapeDtypeStruct((bsz, s_len, 2 * V7X_LANES), BF16)],
        scratch_shapes=[pltpu.VMEM((tm + 2 * F32_SUBLANES, d), BF16),
                        pltpu.VMEM((d // V7X_LANES, tm, V7X_LANES), F32),
                        pltpu.VMEM((1, tm, V7X_LANES), F32)],
        compiler_params=pltpu.CompilerParams(
            dimension_semantics=("parallel", "parallel"), vmem_limit_bytes=V7X_VMEM_LIMIT),
        name="ssd_in",
    )(x, x, x, norm_g.reshape(1, d), w_in, w_dt, conv_w, conv_b.reshape(1, conv_dim), dt_b,
      a_lanes, a_lanes.reshape(V7X_LANES, 1))

    lc = SSD_CHUNK
    cps = min(cps, s_len // lc)
    assert s_len % tm == 0 and tm % (F32_SUBLANES * F32_SUBLANES) == 0 and conv_dim % fc == 0
    assert tm % lc == 0 and s_len % (cps * lc) == 0 and heads % SSD_GROUPS == 0 and 2 * heads <= V7X_LANES // 2
    assert d_inner == heads * SSD_HEAD_DIM and conv_dim == d_inner + 2 * SSD_GROUPS * D_STATE
    nc = s_len // (lc * cps)
    step = lambda w: pl.BlockSpec((1, cps * lc, w), lambda b, c: (b, c, 0))

    xr = lax.broadcasted_iota(jnp.int32, (4 * heads, 2 * d_inner), 0)
    xc = lax.broadcasted_iota(jnp.int32, (4 * heads, 2 * d_inner), 1)
    xsel = ((xr % heads == (xc % d_inner) // SSD_HEAD_DIM)
            & (xr // (2 * heads) == xc // d_inner)).astype(BF16)
    bwd_share = SSD_BWD_INTRA_GROUPS
    state = pltpu.VMEM((SSD_GROUPS, D_STATE, d_inner // SSD_GROUPS), F32)
    rev = lambda w: pl.BlockSpec((1, cps * lc, w), lambda b, c: (b, nc - 1 - c, 0))
    fwd = step
    y_b = pl.pallas_call(
        functools.partial(_ssd_bwd_kernel, d_inner=d_inner, heads=heads, cps=cps,
                          intra_groups=range(0, bwd_share)),
        grid=(bsz, nc),
        in_specs=[rev(conv_dim), rev(V7X_LANES),
                  pl.BlockSpec((1, V7X_LANES, cps * lc), lambda b, c: (b, 0, nc - 1 - c)),
                  rev(2 * V7X_LANES), _full(xsel.shape)],
        out_specs=rev(d_inner),
        out_shape=jax.ShapeDtypeStruct((bsz, s_len, d_inner), F32),
        scratch_shapes=[state],
        compiler_params=pltpu.CompilerParams(
            dimension_semantics=("parallel", "arbitrary"), vmem_limit_bytes=V7X_VMEM_LIMIT),
        name="ssd_scan_bwd",
    )(xbc, acum, tr, ew, xsel)

    return pl.pallas_call(
        functools.partial(_ssd_fwd_kernel, d_inner=d_inner, heads=heads, cps=cps,
                          intra_groups=range(bwd_share, SSD_GROUPS)),
        grid=(bsz, nc),
        in_specs=[fwd(conv_dim), fwd(V7X_LANES),
                  pl.BlockSpec((1, V7X_LANES, cps * lc), lambda b, c: (b, 0, c)),
                  fwd(2 * V7X_LANES), _full(xsel.shape), fwd(d_inner), fwd(d_inner), fwd(d),
                  _full((1, d_inner)), _full((1, d_inner)), _layer(w_out, layer)],
        out_specs=fwd(d),
        out_shape=jax.ShapeDtypeStruct(x.shape, F32),
        scratch_shapes=[state, pltpu.VMEM((cps * lc, d_inner), BF16)],
        compiler_params=pltpu.CompilerParams(
            dimension_semantics=("parallel", "arbitrary"), vmem_limit_bytes=V7X_VMEM_LIMIT),
        name="ssd_scan_fwd",
    )(xbc, acum, tr, ew, xsel, y_b, sz, x,
      jnp.broadcast_to(d_skip.astype(F32)[:, None], (heads, SSD_HEAD_DIM)).reshape(1, d_inner),
      gate_g.reshape(1, d_inner).astype(F32), w_out)


def kernel(x, attn_norm, attn_w_qkv, attn_q_norm, attn_k_norm, attn_sink, attn_w_o, ssd_norm, ssd_w_in, ssd_conv_w, ssd_conv_b, ssd_dt_bias, ssd_a_log, ssd_d, ssd_gate_norm, ssd_w_out, ffn_norm, ffn_w_up, ffn_conv_w, ffn_conv_b, ffn_w_down):
    depth = ffn_norm.shape[0]
    rope = _rope_tables(x.shape[1])
    w_qkv, w_o = attn_w_qkv.astype(BF16), attn_w_o.astype(BF16)
    n_main = ssd_w_out.shape[1] + ssd_conv_w.shape[2]
    w_in, w_dt = ssd_w_in[:, :, :n_main].astype(BF16), ssd_w_in[:, :, n_main:].astype(BF16)
    w_out = ssd_w_out.astype(BF16)
    w_up, w_down = ffn_w_up.astype(BF16), ffn_w_down.astype(BF16)
    for i in range(depth):
        j = i // 2
        if i % 2 == 0:
            x = _window_attention(x, attn_norm[j], w_qkv, attn_q_norm[j], attn_k_norm[j],
                                  attn_sink[j], w_o, j, rope)
        else:
            x = _ssd_mixer(x, ssd_norm[j], w_in, w_dt[j], ssd_conv_w[j], ssd_conv_b[j], ssd_dt_bias[j],
                           ssd_a_log[j], ssd_d[j], ssd_gate_norm[j], w_out, j)
        x = _conv_ffn(x, ffn_norm[i], w_up, ffn_conv_w[i], ffn_conv_b[i], w_down, i)
    return x
```

```python
import functools

import jax
import jax.numpy as jnp
from jax import lax
from jax.experimental import pallas as pl
from jax.experimental.pallas import tpu as pltpu

F32 = jnp.float32
BF16 = jnp.bfloat16

EPS = 1e-6
HEAD_DIM = 64
ROT_DIM = HEAD_DIM // 4
ROPE_THETA = 500000.0
ATTN_BLOCK = 128
SSD_HEAD_DIM = 64
SSD_GROUPS = 8
D_STATE = 128
SSD_CHUNK = 128

V7X_LANES = 128
F32_SUBLANES = 8
V7X_VMEM_BYTES = 64 * 1024 * 1024
V7X_VMEM_LIMIT = V7X_VMEM_BYTES * 7 // 8
NORM_HEAD_SLOTS = 16

FFN_TILE, FFN_COLS = 1024, 256
QKV_TILE, ATTN_TILE = 1024, 512
SSD_IN_TILE, SSD_IN_COLS = 512, 512
SSD_CHUNKS_PER_STEP = 4
SSD_BWD_INTRA_GROUPS = 4


def _rms(x, g):
    return x * lax.rsqrt(jnp.mean(x * x, axis=-1, keepdims=True) + EPS) * g


def _silu(x):
    h = 0.5 * x
    return h + h * jnp.tanh(h)


def _dot(a, b):
    return jnp.dot(a, b, preferred_element_type=F32)


def _dot_nt(a, b):
    return lax.dot_general(a, b, (((1,), (1,)), ((), ())), preferred_element_type=F32)


def _dot_tn(a, b):
    return lax.dot_general(a, b, (((0,), (0,)), ((), ())), preferred_element_type=F32)


def _halo_specs(tm, s_len, d):
    r = tm // F32_SUBLANES
    last = s_len // F32_SUBLANES - 1
    return [
        pl.BlockSpec((1, F32_SUBLANES, d), lambda b, i: (b, jnp.maximum(i * r - 1, 0), 0)),
        pl.BlockSpec((1, tm, d), lambda b, i: (b, i, 0)),
        pl.BlockSpec((1, F32_SUBLANES, d), lambda b, i: (b, jnp.minimum((i + 1) * r, last), 0)),
    ]


def _full(shape):
    return pl.BlockSpec(shape, lambda b, i: (0,) * len(shape), pipeline_mode=pl.Buffered(1))


def _layer(stacked, j):
    return pl.BlockSpec((1,) + stacked.shape[1:], lambda b, i: (j, 0, 0), pipeline_mode=pl.Buffered(1))


def _perm_base(a, tm):
    p = tm // F32_SUBLANES
    t = a * F32_SUBLANES
    return (t % p) * F32_SUBLANES + t // p


def _fill_normed(hn_scr, perm_scr, xp_ref, x_ref, xn_ref, g, tm):
    i = pl.program_id(1)
    n = pl.num_programs(1)
    h = F32_SUBLANES
    keep_p = (i > 0).astype(F32)
    keep_n = (i < n - 1).astype(F32)
    halo = jnp.concatenate([_rms(xp_ref[0], g) * keep_p, _rms(xn_ref[0], g) * keep_n], axis=0)
    hn_scr[0:2 * h, :] = halo.astype(BF16)
    xn = _rms(x_ref[0], g)
    slabs = xn.shape[1] // V7X_LANES
    for a in range(tm // h):
        for j in range(slabs):
            perm_scr[j, pl.ds(_perm_base(a, tm), h, stride=h), :] = (
                xn[a * h:(a + 1) * h, j * V7X_LANES:(j + 1) * V7X_LANES])
    for j in range(slabs):
        hn_scr[2 * h:2 * h + tm, j * V7X_LANES:(j + 1) * V7X_LANES] = perm_scr[j].astype(BF16)


def _unpermute(perm_scr, y, tm):
    h = F32_SUBLANES
    slabs = y.shape[1] // V7X_LANES
    for j in range(slabs):
        perm_scr[j] = y[:, j * V7X_LANES:(j + 1) * V7X_LANES]
    rows = []
    for a in range(tm // h):
        rows.append(jnp.concatenate(
            [perm_scr[j, pl.ds(_perm_base(a, tm), h, stride=h), :] for j in range(slabs)], axis=1))
    return jnp.concatenate(rows, axis=0)


def _dwconv(hh, w, b, tm):
    h = F32_SUBLANES
    k_w = w.shape[0]
    pad = k_w // 2
    hp, hx, hm = hh[0:h], hh[h:2 * h], hh[2 * h:]
    sub = lax.broadcasted_iota(jnp.int32, hp.shape, 0)
    before = []
    for e in range(pad, 0, -1):
        src = jnp.where(sub == h - 1, pltpu.roll(hp, e - 1, 0) if e > 1 else hp, hm[tm - e * h:tm - (e - 1) * h])
        before.append(pltpu.roll(src, 1, 0))
    after = []
    for e in range(pad):
        src = jnp.where(sub == 0, pltpu.roll(hx, h - e, 0) if e > 0 else hx, hm[e * h:(e + 1) * h])
        after.append(pltpu.roll(src, h - 1, 0))
    ext = jnp.concatenate(before + [hm] + after, axis=0)
    y = b + ext[0:tm] * w[0:1]
    for k in range(1, k_w):
        y = y + ext[k * h:k * h + tm] * w[k:k + 1]
    return y


def _ffn_kernel(xp_ref, x_ref, xn_ref, g_ref, wup_ref, cw_ref, cb_ref, wdn_ref, o_ref, hn_scr,
                perm_scr, act_scr, *, tm, fc, d_ff):
    _fill_normed(hn_scr, perm_scr, xp_ref, x_ref, xn_ref, g_ref[...], tm)
    hn = hn_scr[...]
    for c in range(d_ff // fc):
        lo, hi = c * fc, (c + 1) * fc
        hg = _dot(hn, wup_ref[0, :, lo:hi])
        hv = _dot(hn, wup_ref[0, :, d_ff + lo:d_ff + hi])
        gate = _dwconv(hg, cw_ref[:, lo:hi], cb_ref[:, lo:hi], tm)
        val = _dwconv(hv, cw_ref[:, d_ff + lo:d_ff + hi], cb_ref[:, d_ff + lo:d_ff + hi], tm)
        act_scr[:, lo:hi] = (_silu(gate) * val).astype(BF16)
    o_ref[0] = x_ref[0] + _unpermute(perm_scr, _dot(act_scr[...], wdn_ref[0]), tm)


def _conv_ffn(x, g, w_up, conv_w, conv_b, w_down, layer, *, tm=FFN_TILE, fc=FFN_COLS):
    bsz, s_len, d = x.shape
    d_ff = w_down.shape[1]
    tm = min(tm, s_len)
    assert s_len % tm == 0 and tm % (F32_SUBLANES * F32_SUBLANES) == 0 and d_ff % fc == 0
    kern = functools.partial(_ffn_kernel, tm=tm, fc=fc, d_ff=d_ff)
    return pl.pallas_call(
        kern,
        grid=(bsz, s_len // tm),
        in_specs=_halo_specs(tm, s_len, d) + [
            _full((1, d)), _layer(w_up, layer), _full(conv_w.shape), _full((1, 2 * d_ff)),
            _layer(w_down, layer)],
        out_specs=pl.BlockSpec((1, tm, d), lambda b, i: (b, i, 0)),
        out_shape=jax.ShapeDtypeStruct(x.shape, F32),
        scratch_shapes=[pltpu.VMEM((tm + 2 * F32_SUBLANES, d), BF16),
                        pltpu.VMEM((d // V7X_LANES, tm, V7X_LANES), F32),
                        pltpu.VMEM((tm, d_ff), BF16)],
        compiler_params=pltpu.CompilerParams(
            dimension_semantics=("parallel", "parallel"), vmem_limit_bytes=V7X_VMEM_LIMIT),
        name="conv_ffn",
    )(x, x, x, g.reshape(1, d), w_up, conv_w, conv_b.reshape(1, -1), w_down)


def _rope_tables(s_len):
    half = ROT_DIM // 2
    pos = jnp.arange(s_len, dtype=F32)
    inv_freq = ROPE_THETA ** (-(jnp.arange(0, ROT_DIM, 2, dtype=F32) / ROT_DIM))
    ang = pos[:, None] * inv_freq[None, :]
    cos, sin = jnp.cos(ang), jnp.sin(ang)
    rest = HEAD_DIM - ROT_DIM
    c = jnp.concatenate([cos, cos, jnp.ones((s_len, rest), F32)], axis=1)
    s1 = jnp.concatenate([-sin, jnp.zeros((s_len, half + rest), F32)], axis=1)
    s2 = jnp.concatenate([jnp.zeros((s_len, half), F32), sin, jnp.zeros((s_len, rest), F32)], axis=1)
    rep = V7X_LANES // HEAD_DIM
    return jnp.tile(c, (1, rep)), jnp.tile(s1, (1, rep)), jnp.tile(s2, (1, rep))


def _norm_rope(t, red, expd, g, c, s1, s2, scale):
    half = ROT_DIM // 2
    rs = lax.rsqrt(_dot((t * t).astype(BF16), red) + EPS)
    hi = rs.astype(BF16).astype(F32)
    lane = lax.broadcasted_iota(jnp.int32, rs.shape, 1)
    packed = jnp.where(lane < NORM_HEAD_SLOTS, hi, pltpu.roll(rs - hi, NORM_HEAD_SLOTS, 1))
    tn = t * _dot(packed.astype(BF16), expd) * g
    outs = []
    for j in range(t.shape[1] // V7X_LANES):
        tc = tn[:, j * V7X_LANES:(j + 1) * V7X_LANES]
        tr = tc * c + pltpu.roll(tc, V7X_LANES - half, 1) * s1 + pltpu.roll(tc, half, 1) * s2
        outs.append(tr * scale if scale != 1.0 else tr)
    return jnp.concatenate(outs, axis=1)


def _replicate_heads(t, group):
    lane = lax.broadcasted_iota(jnp.int32, (t.shape[0], V7X_LANES), 1)
    first = lane < HEAD_DIM
    outs = []
    for j in range(t.shape[1] // V7X_LANES):
        tc = t[:, j * V7X_LANES:(j + 1) * V7X_LANES]
        tc_sw = pltpu.roll(tc, HEAD_DIM, 1)
        even = jnp.where(first, tc, tc_sw)
        odd = jnp.where(first, tc_sw, tc)
        outs += [even] * (group // 2) + [odd] * (group // 2)
    return jnp.concatenate(outs, axis=1)


def _qkv_kernel(x_ref, g_ref, w_ref, qg_ref, kg_ref, redq_ref, expq_ref, redk_ref, expk_ref,
                c_ref, s1_ref, s2_ref, q_out, k_out, v_out, *, q_dim, kv_dim, group):
    hn = _rms(x_ref[0], g_ref[...]).astype(BF16)
    c, s1, s2 = c_ref[...], s1_ref[...], s2_ref[...]
    q = _dot(hn, w_ref[0, :, 0:q_dim])
    q_out[0] = _norm_rope(q, redq_ref[...], expq_ref[...], qg_ref[...], c, s1, s2,
                          HEAD_DIM ** -0.5).astype(BF16)
    k = _dot(hn, w_ref[0, :, q_dim:q_dim + kv_dim])
    kr = _norm_rope(k, redk_ref[...], expk_ref[...], kg_ref[...], c, s1, s2, 1.0)
    k_out[0] = _replicate_heads(kr, group).astype(BF16)
    v = _dot(hn, w_ref[0, :, q_dim + kv_dim:q_dim + 2 * kv_dim])
    v_out[0] = _replicate_heads(v, group).astype(BF16)


def _attn_kernel(sink_ref, q_ref, kp_ref, k_ref, kn_ref, vp_ref, v_ref, vn_ref, x_ref, wo_ref,
                 o_ref, o_scr, *, tq, n_kv, group):
    i = pl.program_id(1)
    n = pl.num_programs(1)
    blk = ATTN_BLOCK
    gw = group * HEAD_DIM
    k_all = jnp.concatenate([kp_ref[0], k_ref[0], kn_ref[0]], axis=0)
    v_all = jnp.concatenate([vp_ref[0], v_ref[0], vn_ref[0]], axis=0)
    row = lax.broadcasted_iota(jnp.int32, (group * blk, 3 * blk), 0) % blk
    col = lax.broadcasted_iota(jnp.int32, (group * blk, 3 * blk), 1)
    band = (col >= row) & (col <= row + 2 * blk)
    head_of_lane = lax.broadcasted_iota(jnp.int32, (blk, gw), 1) // HEAD_DIM
    head_of_row = lax.broadcasted_iota(jnp.int32, (group * blk, 1), 0) // blk
    nblk = tq // blk
    for jb in range(nblk):
        valid = band
        if jb == 0:
            valid = valid & ((col >= blk) | (i > 0))
        if jb == nblk - 1:
            valid = valid & ((col < 2 * blk) | (i < n - 1))
        for kh in range(n_kv):
            qg = q_ref[0, jb * blk:(jb + 1) * blk, kh * gw:(kh + 1) * gw]
            kw = k_all[jb * blk:(jb + 3) * blk, kh * gw:(kh + 1) * gw]
            vw = v_all[jb * blk:(jb + 3) * blk, kh * gw:(kh + 1) * gw]
            qs = jnp.concatenate(
                [jnp.where(head_of_lane == hl, qg, jnp.zeros_like(qg)) for hl in range(group)], axis=0)
            sink = jnp.full((group * blk, 1), sink_ref[kh * group + group - 1], F32)
            for hl in range(group - 2, -1, -1):
                sink = jnp.where(head_of_row == hl, sink_ref[kh * group + hl], sink)
            s = jnp.where(valid, _dot_nt(qs, kw), -1e30)
            m = jnp.maximum(jnp.max(s, axis=-1, keepdims=True), sink)
            p = jnp.exp(s - m)
            denom = jnp.sum(p, axis=-1, keepdims=True) + jnp.exp(sink - m)
            o = _dot(p.astype(BF16), vw) / denom
            og = o[(group - 1) * blk:group * blk]
            for hl in range(group - 2, -1, -1):
                og = jnp.where(head_of_lane == hl, o[hl * blk:(hl + 1) * blk], og)
            o_scr[jb * blk:(jb + 1) * blk, kh * gw:(kh + 1) * gw] = og.astype(BF16)
    o_ref[0] = x_ref[0] + _dot(o_scr[...], wo_ref[0])


def _window_attention(x, norm_g, w_qkv, q_g, k_g, sink, w_o, layer, rope,
                      *, tm=QKV_TILE, tq=ATTN_TILE):
    bsz, s_len, d = x.shape
    n_heads = sink.shape[0]
    q_dim = w_o.shape[1]
    kv_dim = (w_qkv.shape[2] - q_dim) // 2
    n_kv = kv_dim // HEAD_DIM
    group = n_heads // n_kv
    tm = min(tm, s_len)
    tq = min(tq, s_len)
    assert s_len % tm == 0 and s_len % tq == 0 and tq % ATTN_BLOCK == 0
    assert n_heads <= NORM_HEAD_SLOTS and group % 2 == 0 and q_dim == n_heads * HEAD_DIM
    c, s1, s2 = rope

    def reduce_expand(width):
        i = lax.broadcasted_iota(jnp.int32, (width, V7X_LANES), 0)
        j = lax.broadcasted_iota(jnp.int32, (width, V7X_LANES), 1)
        red = jnp.where(i // HEAD_DIM == j, 1.0 / HEAD_DIM, 0.0).astype(BF16)
        expd = ((j.T % NORM_HEAD_SLOTS == i.T // HEAD_DIM) & (j.T < 2 * NORM_HEAD_SLOTS)).astype(BF16)
        return red, expd

    redq, expq = reduce_expand(q_dim)
    redk, expk = reduce_expand(kv_dim)
    tile = lambda w: pl.BlockSpec((1, tm, w), lambda b, i: (b, i, 0))
    rope_spec = pl.BlockSpec((tm, V7X_LANES), lambda b, i: (i, 0))
    q, k, v = pl.pallas_call(
        functools.partial(_qkv_kernel, q_dim=q_dim, kv_dim=kv_dim, group=group),
        grid=(bsz, s_len // tm),
        in_specs=[tile(d), _full((1, d)), _layer(w_qkv, layer), _full((1, q_dim)),
                  _full((1, kv_dim)), _full(redq.shape), _full(expq.shape), _full(redk.shape),
                  _full(expk.shape), rope_spec, rope_spec, rope_spec],
        out_specs=[tile(q_dim), tile(q_dim), tile(q_dim)],
        out_shape=[jax.ShapeDtypeStruct((bsz, s_len, q_dim), BF16)] * 3,
        compiler_params=pltpu.CompilerParams(
            dimension_semantics=("parallel", "parallel"), vmem_limit_bytes=V7X_VMEM_LIMIT),
        name="attn_qkv",
    )(x, norm_g.reshape(1, d), w_qkv, jnp.tile(q_g, n_heads).reshape(1, q_dim),
      jnp.tile(k_g, n_kv).reshape(1, kv_dim), redq, expq, redk, expk, c, s1, s2)

    r = tq // ATTN_BLOCK
    last = s_len // ATTN_BLOCK - 1
    prev_spec = pl.BlockSpec((1, ATTN_BLOCK, q_dim), lambda b, i: (b, jnp.maximum(i * r - 1, 0), 0))
    main_spec = pl.BlockSpec((1, tq, q_dim), lambda b, i: (b, i, 0))
    next_spec = pl.BlockSpec((1, ATTN_BLOCK, q_dim), lambda b, i: (b, jnp.minimum((i + 1) * r, last), 0))
    return pl.pallas_call(
        functools.partial(_attn_kernel, tq=tq, n_kv=n_kv, group=group),
        grid=(bsz, s_len // tq),
        in_specs=[pl.BlockSpec(memory_space=pltpu.SMEM), main_spec,
                  prev_spec, main_spec, next_spec, prev_spec, main_spec, next_spec,
                  pl.BlockSpec((1, tq, d), lambda b, i: (b, i, 0)), _layer(w_o, layer)],
        out_specs=pl.BlockSpec((1, tq, d), lambda b, i: (b, i, 0)),
        out_shape=jax.ShapeDtypeStruct(x.shape, F32),
        scratch_shapes=[pltpu.VMEM((tq, q_dim), BF16)],
        compiler_params=pltpu.CompilerParams(
            dimension_semantics=("parallel", "parallel"), vmem_limit_bytes=V7X_VMEM_LIMIT),
        name="attn_core",
    )(sink.astype(F32), q, k, k, k, v, v, v, x, w_o)


def _softplus(x):
    return jnp.maximum(x, 0.0) + jnp.log1p(jnp.exp(-jnp.abs(x)))


def _ssd_in_kernel(xp_ref, x_ref, xn_ref, g_ref, w_ref, wdt_ref, cw_ref, cb_ref, dtb_ref, a_ref, at_ref,
                   z_out, xbc_out, acum_out, tr_out, ew_out, hn_scr, perm_scr, dt_scr,
                   *, tm, fc, d_inner, conv_dim, heads):
    g = g_ref[...]
    _fill_normed(hn_scr, perm_scr, xp_ref, x_ref, xn_ref, g, tm)
    hn_nat = _rms(x_ref[0], g).astype(BF16)
    z_out[0] = _silu(_dot(hn_nat, w_ref[0, :, 0:d_inner])).astype(BF16)
    dt_scr[0] = _softplus(_dot(hn_nat, wdt_ref[...]) + dtb_ref[...])
    _ssd_decay_terms(dt_scr, a_ref, at_ref, acum_out, tr_out, ew_out, heads, tm // SSD_CHUNK)
    hn = hn_scr[...]
    for c in range(conv_dim // fc):
        lo, hi = c * fc, (c + 1) * fc
        h = _dot(hn, w_ref[0, :, d_inner + lo:d_inner + hi])
        y = _unpermute(perm_scr, _dwconv(h, cw_ref[:, lo:hi], cb_ref[:, lo:hi], tm), tm)
        xbc_out[0, :, lo:hi] = _silu(y).astype(BF16)


def _pieces(x, n):
    out = []
    for _ in range(n - 1):
        p = x.astype(BF16).astype(F32)
        out.append(p)
        x = x - p
    out.append(x.astype(BF16).astype(F32))
    return out


def _ssd_decay_terms(dt_ref, a_ref, at_ref, acum_out, tr_out, ew_out, heads, n_chunks):
    lc = SSD_CHUNK
    row = lax.broadcasted_iota(jnp.int32, (lc, lc), 0)
    col = lax.broadcasted_iota(jnp.int32, (lc, lc), 1)
    lower = (row >= col).astype(BF16)
    upper = (row <= col).astype(BF16)
    fwd_lane = lax.broadcasted_iota(jnp.int32, (lc, V7X_LANES), 1) < heads
    fwd_row = lax.broadcasted_iota(jnp.int32, (V7X_LANES, lc), 0) < heads
    for k in range(n_chunks):
        rows = slice(k * lc, (k + 1) * lc)
        dt = dt_ref[0, rows, :]
        dtt = dt.T
        ps = [p.astype(BF16) for p in _pieces(dt * a_ref[...], 3)]
        cum_f = (_dot(lower, ps[0]) + _dot(lower, ps[1])) + _dot(lower, ps[2])
        cum_b = (_dot(upper, ps[0]) + _dot(upper, ps[1])) + _dot(upper, ps[2])
        a_cum = jnp.where(fwd_lane, cum_f, cum_b)
        qs = [q.astype(BF16) for q in _pieces(dtt * at_ref[...], 3)]
        cum_tf = (_dot(qs[0], upper) + _dot(qs[1], upper)) + _dot(qs[2], upper)
        cum_tb = (_dot(qs[0], lower) + _dot(qs[1], lower)) + _dot(qs[2], lower)
        a_cum_t = jnp.where(fwd_row, cum_tf, cum_tb)
        a_end = jnp.where(fwd_lane[0:1], a_cum[lc - 1:lc, :], a_cum[0:1, :])
        e1, e2 = _pieces(jnp.exp(a_cum), 2)
        w1, w2 = _pieces(dt * jnp.exp(a_end - a_cum), 2)
        acum_out[0, rows, :] = a_cum
        half = V7X_LANES // 2
        dsum = jnp.log(dtt[0:heads] + dtt[heads:2 * heads])
        tr_out[0, :, rows] = jnp.concatenate(
            [a_cum_t[0:half] - jnp.log(dtt[0:half]), dsum, jnp.zeros((half - heads, lc), F32)], axis=0)
        for d in range(2):
            lanes = slice(d * heads, (d + 1) * heads)
            ew_out[0, rows, d * V7X_LANES:(d + 1) * V7X_LANES] = jnp.concatenate(
                [e1[:, lanes], e2[:, lanes], w1[:, lanes], w2[:, lanes]], axis=1).astype(BF16)


def _ssd_state_step(xbc_ref, ew, state_ref, rows, g, end, d_inner):
    gw = d_inner // SSD_GROUPS
    gn = SSD_GROUPS * D_STATE
    e_exp = ew[:, g * gw:(g + 1) * gw]
    w_exp = ew[:, d_inner + g * gw:d_inner + (g + 1) * gw]
    b_g = xbc_ref[0, rows, d_inner + g * D_STATE:d_inner + (g + 1) * D_STATE]
    c_g = xbc_ref[0, rows, d_inner + gn + g * D_STATE:d_inner + gn + (g + 1) * D_STATE]
    x_g = xbc_ref[0, rows, g * gw:(g + 1) * gw]
    st = state_ref[g]
    y_off = _dot(c_g, st.astype(BF16)) * e_exp
    xw = (x_g.astype(F32) * w_exp).astype(BF16)
    state_ref[g] = e_exp[end:end + 1, :] * st + _dot_tn(b_g, xw)
    return y_off, b_g, c_g, x_g


def _ssd_intra(acum_ref, tr_ref, rows, g, b_g, c_g, x_g, heads, masks):
    from_fwd, from_bwd, lane_head = masks
    hpg = heads // SSD_GROUPS
    cb = _dot_nt(c_g, b_g)
    ms, xb = [], []
    for r in range(hpg):
        h = g * hpg + r
        zf = acum_ref[0, rows, h:h + 1] - tr_ref[0, h:h + 1, rows]
        zb = acum_ref[0, rows, heads + h:heads + h + 1] - tr_ref[0, heads + h:heads + h + 1, rows]
        zd = tr_ref[0, 2 * heads + h:2 * heads + h + 1, rows]
        z = jnp.where(from_fwd, zf, jnp.where(from_bwd, zb, zd))
        ms.append((cb * jnp.exp(z)).astype(BF16))
        xb.append(jnp.where(lane_head == r, x_g, jnp.zeros_like(x_g)))
    return _dot(jnp.concatenate(ms, axis=1), jnp.concatenate(xb, axis=0))


def _intra_masks(lc, gw):
    row = lax.broadcasted_iota(jnp.int32, (lc, lc), 0)
    col = lax.broadcasted_iota(jnp.int32, (lc, lc), 1)
    lane_head = lax.broadcasted_iota(jnp.int32, (lc, gw), 1) // SSD_HEAD_DIM
    return col < row, col > row, lane_head


def _ssd_bwd_kernel(xbc_ref, acum_ref, tr_ref, ew_ref, xsel_ref, y_out, state_ref,
                    *, d_inner, heads, cps, intra_groups):
    @pl.when(pl.program_id(1) == 0)
    def _():
        state_ref[...] = jnp.zeros_like(state_ref)

    lc = SSD_CHUNK
    gw = d_inner // SSD_GROUPS
    masks = _intra_masks(lc, gw)
    for k in range(cps - 1, -1, -1):
        rows = slice(k * lc, (k + 1) * lc)
        ew = _dot(ew_ref[0, rows, V7X_LANES:2 * V7X_LANES], xsel_ref[...])
        for g in range(SSD_GROUPS):
            y, b_g, c_g, x_g = _ssd_state_step(xbc_ref, ew, state_ref, rows, g, 0, d_inner)
            if g in intra_groups:
                y = y + _ssd_intra(acum_ref, tr_ref, rows, g, b_g, c_g, x_g, heads, masks)
            y_out[0, rows, g * gw:(g + 1) * gw] = y


def _ssd_fwd_kernel(xbc_ref, acum_ref, tr_ref, ew_ref, xsel_ref, yb_ref, z_ref, x_ref,
                    dexp_ref, gg_ref, wout_ref, o_ref, state_ref, yn_scr,
                    *, d_inner, heads, cps, intra_groups):
    @pl.when(pl.program_id(1) == 0)
    def _():
        state_ref[...] = jnp.zeros_like(state_ref)

    gw = d_inner // SSD_GROUPS
    lc = SSD_CHUNK
    masks = _intra_masks(lc, gw)
    for k in range(cps):
        rows = slice(k * lc, (k + 1) * lc)
        ew = _dot(ew_ref[0, rows, 0:V7X_LANES], xsel_ref[...])
        for g in range(SSD_GROUPS):
            cols = slice(g * gw, (g + 1) * gw)
            y, b_g, c_g, x_g = _ssd_state_step(xbc_ref, ew, state_ref, rows, g, lc - 1, d_inner)
            if g in intra_groups:
                y = y + _ssd_intra(acum_ref, tr_ref, rows, g, b_g, c_g, x_g, heads, masks)
            y = y + yb_ref[0, rows, cols] + x_g.astype(F32) * dexp_ref[:, cols]
            y = y * z_ref[0, rows, cols].astype(F32)
            y = y * lax.rsqrt(jnp.mean(y * y, axis=-1, keepdims=True) + EPS)
            yn_scr[rows, cols] = (y * gg_ref[:, cols]).astype(BF16)
    o_ref[0] = x_ref[0] + _dot(yn_scr[...], wout_ref[0])


def _ssd_mixer(x, norm_g, w_in, conv_w, conv_b, dt_bias, a_log, d_skip, gate_g, w_out, layer,
               *, tm=SSD_IN_TILE, fc=SSD_IN_COLS, cps=SSD_CHUNKS_PER_STEP):
    bsz, s_len, d = x.shape
    d_inner = w_out.shape[1]
    heads = d_skip.shape[0]
    conv_dim = conv_w.shape[1]
    tm = min(tm, s_len)
    dt_pad = V7X_LANES - 2 * heads
    w_dt = jnp.pad(w_in[layer, :, d_inner + conv_dim:], ((0, 0), (0, dt_pad)))
    dt_b = jnp.pad(dt_bias.reshape(1, 2 * heads).astype(F32), ((0, 0), (0, dt_pad)))
    a_lanes = jnp.pad((-jnp.exp(a_log.astype(F32))).reshape(1, 2 * heads), ((0, 0), (0, dt_pad)))
    tile = lambda w: pl.BlockSpec((1, tm, w), lambda b, i: (b, i, 0))
    sz, xbc, acum, tr, ew = pl.pallas_call(
        functools.partial(_ssd_in_kernel, tm=tm, fc=fc, d_inner=d_inner, conv_dim=conv_dim, heads=heads),
        grid=(bsz, s_len // tm),
        in_specs=_halo_specs(tm, s_len, d) + [
            _full((1, d)), _layer(w_in, layer), _full((d, V7X_LANES)), _full(conv_w.shape),
            _full((1, conv_dim)), _full((1, V7X_LANES)), _full((1, V7X_LANES)), _full((V7X_LANES, 1))],
        out_specs=[tile(d_inner), tile(conv_dim), tile(V7X_LANES),
                   pl.BlockSpec((1, V7X_LANES, tm), lambda b, i: (b, 0, i)), tile(2 * V7X_LANES)],
        out_shape=[jax.ShapeDtypeStruct((bsz, s_len, d_inner), BF16),
                   jax.ShapeDtypeStruct((bsz, s_len, conv_dim), BF16),
                   jax.ShapeDtypeStruct((bsz, s_len, V7X_LANES), F32),
                   jax.ShapeDtypeStruct((bsz, V7X_LANES, s_len), F32),
                   jax.ShapeDtypeStruct((bsz, s_len, 2 * V7X_LANES), BF16)],
        scratch_shapes=[pltpu.VMEM((tm + 2 * F32_SUBLANES, d), BF16),
                        pltpu.VMEM((d // V7X_LANES, tm, V7X_LANES), F32),
                        pltpu.VMEM((1, tm, V7X_LANES), F32)],
        compiler_params=pltpu.CompilerParams(
            dimension_semantics=("parallel", "parallel"), vmem_limit_bytes=V7X_VMEM_LIMIT),
        name="ssd_in",
    )(x, x, x, norm_g.reshape(1, d), w_in, w_dt, conv_w, conv_b.reshape(1, conv_dim), dt_b,
      a_lanes, a_lanes.reshape(V7X_LANES, 1))

    lc = SSD_CHUNK
    cps = min(cps, s_len // lc)
    assert s_len % tm == 0 and tm % (F32_SUBLANES * F32_SUBLANES) == 0 and conv_dim % fc == 0
    assert tm % lc == 0 and s_len % (cps * lc) == 0 and heads % SSD_GROUPS == 0 and 2 * heads <= V7X_LANES // 2
    assert d_inner == heads * SSD_HEAD_DIM and conv_dim == d_inner + 2 * SSD_GROUPS * D_STATE
    nc = s_len // (lc * cps)
    step = lambda w: pl.BlockSpec((1, cps * lc, w), lambda b, c: (b, c, 0))

    xr = lax.broadcasted_iota(jnp.int32, (4 * heads, 2 * d_inner), 0)
    xc = lax.broadcasted_iota(jnp.int32, (4 * heads, 2 * d_inner), 1)
    xsel = ((xr % heads == (xc % d_inner) // SSD_HEAD_DIM)
            & (xr // (2 * heads) == xc // d_inner)).astype(BF16)
    bwd_share = SSD_BWD_INTRA_GROUPS
    state = pltpu.VMEM((SSD_GROUPS, D_STATE, d_inner // SSD_GROUPS), F32)
    rev = lambda w: pl.BlockSpec((1, cps * lc, w), lambda b, c: (b, nc - 1 - c, 0))
    fwd = step
    y_b = pl.pallas_call(
        functools.partial(_ssd_bwd_kernel, d_inner=d_inner, heads=heads, cps=cps,
                          intra_groups=range(0, bwd_share)),
        grid=(bsz, nc),
        in_specs=[rev(conv_dim), rev(V7X_LANES),
                  pl.BlockSpec((1, V7X_LANES, cps * lc), lambda b, c: (b, 0, nc - 1 - c)),
                  rev(2 * V7X_LANES), _full(xsel.shape)],
        out_specs=rev(d_inner),
        out_shape=jax.ShapeDtypeStruct((bsz, s_len, d_inner), F32),
        scratch_shapes=[state],
        compiler_params=pltpu.CompilerParams(
            dimension_semantics=("parallel", "arbitrary"), vmem_limit_bytes=V7X_VMEM_LIMIT),
        name="ssd_scan_bwd",
    )(xbc, acum, tr, ew, xsel)

    return pl.pallas_call(
        functools.partial(_ssd_fwd_kernel, d_inner=d_inner, heads=heads, cps=cps,
                          intra_groups=range(bwd_share, SSD_GROUPS)),
        grid=(bsz, nc),
        in_specs=[fwd(conv_dim), fwd(V7X_LANES),
                  pl.BlockSpec((1, V7X_LANES, cps * lc), lambda b, c: (b, 0, c)),
                  fwd(2 * V7X_LANES), _full(xsel.shape), fwd(d_inner), fwd(d_inner), fwd(d),
                  _full((1, d_inner)), _full((1, d_inner)), _layer(w_out, layer)],
        out_specs=fwd(d),
        out_shape=jax.ShapeDtypeStruct(x.shape, F32),
        scratch_shapes=[state, pltpu.VMEM((cps * lc, d_inner), BF16)],
        compiler_params=pltpu.CompilerParams(
            dimension_semantics=("parallel", "arbitrary"), vmem_limit_bytes=V7X_VMEM_LIMIT),
        name="ssd_scan_fwd",
    )(xbc, acum, tr, ew, xsel, y_b, sz, x,
      jnp.broadcast_to(d_skip.astype(F32)[:, None], (heads, SSD_HEAD_DIM)).reshape(1, d_inner),
      gate_g.reshape(1, d_inner).astype(F32), w_out)


def kernel(x, attn_norm, attn_w_qkv, attn_q_norm, attn_k_norm, attn_sink, attn_w_o, ssd_norm, ssd_w_in, ssd_conv_w, ssd_conv_b, ssd_dt_bias, ssd_a_log, ssd_d, ssd_gate_norm, ssd_w_out, ffn_norm, ffn_w_up, ffn_conv_w, ffn_conv_b, ffn_w_down):
    depth = ffn_norm.shape[0]
    rope = _rope_tables(x.shape[1])
    w_qkv, w_o = attn_w_qkv.astype(BF16), attn_w_o.astype(BF16)
    w_in, w_out = ssd_w_in.astype(BF16), ssd_w_out.astype(BF16)
    w_up, w_down = ffn_w_up.astype(BF16), ffn_w_down.astype(BF16)
    for i in range(depth):
        j = i // 2
        if i % 2 == 0:
            x = _window_attention(x, attn_norm[j], w_qkv, attn_q_norm[j], attn_k_norm[j],
                                  attn_sink[j], w_o, j, rope)
        else:
            x = _ssd_mixer(x, ssd_norm[j], w_in, ssd_conv_w[j], ssd_conv_b[j], ssd_dt_bias[j],
                           ssd_a_log[j], ssd_d[j], ssd_gate_norm[j], w_out, j)
        x = _conv_ffn(x, ffn_norm[i], w_up, ffn_conv_w[i], ffn_conv_b[i], w_down, i)
    return x
```

```python
import functools

import jax
import jax.numpy as jnp
from jax import lax
from jax.experimental import pallas as pl
from jax.experimental.pallas import tpu as pltpu

F32 = jnp.float32
BF16 = jnp.bfloat16

EPS = 1e-6
HEAD_DIM = 64
ROT_DIM = HEAD_DIM // 4
ROPE_THETA = 500000.0
ATTN_BLOCK = 128
SSD_HEAD_DIM = 64
SSD_GROUPS = 8
D_STATE = 128
SSD_CHUNK = 128

V7X_LANES = 128
F32_SUBLANES = 8
V7X_VMEM_BYTES = 64 * 1024 * 1024
V7X_VMEM_LIMIT = V7X_VMEM_BYTES * 7 // 8
NORM_HEAD_SLOTS = 16

FFN_TILE, FFN_COLS = 1024, 256
QKV_TILE, ATTN_TILE = 1024, 512
SSD_IN_TILE, SSD_IN_COLS = 512, 512
SSD_CHUNKS_PER_STEP = 4
SSD_BWD_INTRA_GROUPS = 4


def _rms(x, g):
    return x * lax.rsqrt(jnp.mean(x * x, axis=-1, keepdims=True) + EPS) * g


def _silu(x):
    h = 0.5 * x
    return h + h * jnp.tanh(h)


def _dot(a, b):
    return jnp.dot(a, b, preferred_element_type=F32)


def _dot_nt(a, b):
    return lax.dot_general(a, b, (((1,), (1,)), ((), ())), preferred_element_type=F32)


def _dot_tn(a, b):
    return lax.dot_general(a, b, (((0,), (0,)), ((), ())), preferred_element_type=F32)


def _halo_specs(tm, s_len, d):
    r = tm // F32_SUBLANES
    last = s_len // F32_SUBLANES - 1
    return [
        pl.BlockSpec((1, F32_SUBLANES, d), lambda b, i: (b, jnp.maximum(i * r - 1, 0), 0)),
        pl.BlockSpec((1, tm, d), lambda b, i: (b, i, 0)),
        pl.BlockSpec((1, F32_SUBLANES, d), lambda b, i: (b, jnp.minimum((i + 1) * r, last), 0)),
    ]


def _full(shape):
    return pl.BlockSpec(shape, lambda b, i: (0,) * len(shape), pipeline_mode=pl.Buffered(1))


def _layer(stacked, j):
    return pl.BlockSpec((1,) + stacked.shape[1:], lambda b, i: (j, 0, 0), pipeline_mode=pl.Buffered(1))


def _perm_base(a, tm):
    p = tm // F32_SUBLANES
    t = a * F32_SUBLANES
    return (t % p) * F32_SUBLANES + t // p


def _fill_normed(hn_scr, perm_scr, xp_ref, x_ref, xn_ref, g, tm):
    i = pl.program_id(1)
    n = pl.num_programs(1)
    h = F32_SUBLANES
    keep_p = (i > 0).astype(F32)
    keep_n = (i < n - 1).astype(F32)
    halo = jnp.concatenate([_rms(xp_ref[0], g) * keep_p, _rms(xn_ref[0], g) * keep_n], axis=0)
    hn_scr[0:2 * h, :] = halo.astype(BF16)
    xn = _rms(x_ref[0], g)
    slabs = xn.shape[1] // V7X_LANES
    for a in range(tm // h):
        for j in range(slabs):
            perm_scr[j, pl.ds(_perm_base(a, tm), h, stride=h), :] = (
                xn[a * h:(a + 1) * h, j * V7X_LANES:(j + 1) * V7X_LANES])
    for j in range(slabs):
        hn_scr[2 * h:2 * h + tm, j * V7X_LANES:(j + 1) * V7X_LANES] = perm_scr[j].astype(BF16)


def _unpermute(perm_scr, y, tm):
    h = F32_SUBLANES
    slabs = y.shape[1] // V7X_LANES
    for j in range(slabs):
        perm_scr[j] = y[:, j * V7X_LANES:(j + 1) * V7X_LANES]
    rows = []
    for a in range(tm // h):
        rows.append(jnp.concatenate(
            [perm_scr[j, pl.ds(_perm_base(a, tm), h, stride=h), :] for j in range(slabs)], axis=1))
    return jnp.concatenate(rows, axis=0)


def _dwconv(hh, w, b, tm):
    h = F32_SUBLANES
    k_w = w.shape[0]
    pad = k_w // 2
    hp, hx, hm = hh[0:h], hh[h:2 * h], hh[2 * h:]
    sub = lax.broadcasted_iota(jnp.int32, hp.shape, 0)
    before = []
    for e in range(pad, 0, -1):
        src = jnp.where(sub == h - 1, pltpu.roll(hp, e - 1, 0) if e > 1 else hp, hm[tm - e * h:tm - (e - 1) * h])
        before.append(pltpu.roll(src, 1, 0))
    after = []
    for e in range(pad):
        src = jnp.where(sub == 0, pltpu.roll(hx, h - e, 0) if e > 0 else hx, hm[e * h:(e + 1) * h])
        after.append(pltpu.roll(src, h - 1, 0))
    ext = jnp.concatenate(before + [hm] + after, axis=0)
    y = b + ext[0:tm] * w[0:1]
    for k in range(1, k_w):
        y = y + ext[k * h:k * h + tm] * w[k:k + 1]
    return y


def _ffn_kernel(xp_ref, x_ref, xn_ref, g_ref, wup_ref, cw_ref, cb_ref, wdn_ref, o_ref, hn_scr,
                perm_scr, act_scr, *, tm, fc, d_ff):
    _fill_normed(hn_scr, perm_scr, xp_ref, x_ref, xn_ref, g_ref[...], tm)
    hn = hn_scr[...]
    for c in range(d_ff // fc):
        lo, hi = c * fc, (c + 1) * fc
        hg = _dot(hn, wup_ref[0, :, lo:hi])
        hv = _dot(hn, wup_ref[0, :, d_ff + lo:d_ff + hi])
        gate = _dwconv(hg, cw_ref[:, lo:hi], cb_ref[:, lo:hi], tm)
        val = _dwconv(hv, cw_ref[:, d_ff + lo:d_ff + hi], cb_ref[:, d_ff + lo:d_ff + hi], tm)
        act_scr[:, lo:hi] = (_silu(gate) * val).astype(BF16)
    o_ref[0] = x_ref[0] + _unpermute(perm_scr, _dot(act_scr[...], wdn_ref[0]), tm)


def _conv_ffn(x, g, w_up, conv_w, conv_b, w_down, layer, *, tm=FFN_TILE, fc=FFN_COLS):
    bsz, s_len, d = x.shape
    d_ff = w_down.shape[1]
    tm = min(tm, s_len)
    assert s_len % tm == 0 and tm % (F32_SUBLANES * F32_SUBLANES) == 0 and d_ff % fc == 0
    kern = functools.partial(_ffn_kernel, tm=tm, fc=fc, d_ff=d_ff)
    return pl.pallas_call(
        kern,
        grid=(bsz, s_len // tm),
        in_specs=_halo_specs(tm, s_len, d) + [
            _full((1, d)), _layer(w_up, layer), _full(conv_w.shape), _full((1, 2 * d_ff)),
            _layer(w_down, layer)],
        out_specs=pl.BlockSpec((1, tm, d), lambda b, i: (b, i, 0)),
        out_shape=jax.ShapeDtypeStruct(x.shape, F32),
        scratch_shapes=[pltpu.VMEM((tm + 2 * F32_SUBLANES, d), BF16),
                        pltpu.VMEM((d // V7X_LANES, tm, V7X_LANES), F32),
                        pltpu.VMEM((tm, d_ff), BF16)],
        compiler_params=pltpu.CompilerParams(
            dimension_semantics=("parallel", "parallel"), vmem_limit_bytes=V7X_VMEM_LIMIT),
        name="conv_ffn",
    )(x, x, x, g.reshape(1, d), w_up, conv_w, conv_b.reshape(1, -1), w_down)


def _rope_tables(s_len):
    half = ROT_DIM // 2
    pos = jnp.arange(s_len, dtype=F32)
    inv_freq = ROPE_THETA ** (-(jnp.arange(0, ROT_DIM, 2, dtype=F32) / ROT_DIM))
    ang = pos[:, None] * inv_freq[None, :]
    cos, sin = jnp.cos(ang), jnp.sin(ang)
    rest = HEAD_DIM - ROT_DIM
    c = jnp.concatenate([cos, cos, jnp.ones((s_len, rest), F32)], axis=1)
    s1 = jnp.concatenate([-sin, jnp.zeros((s_len, half + rest), F32)], axis=1)
    s2 = jnp.concatenate([jnp.zeros((s_len, half), F32), sin, jnp.zeros((s_len, rest), F32)], axis=1)
    rep = V7X_LANES // HEAD_DIM
    return jnp.tile(c, (1, rep)), jnp.tile(s1, (1, rep)), jnp.tile(s2, (1, rep))


def _norm_rope(t, red, expd, g, c, s1, s2, scale):
    half = ROT_DIM // 2
    rs = lax.rsqrt(_dot((t * t).astype(BF16), red) + EPS)
    hi = rs.astype(BF16).astype(F32)
    lane = lax.broadcasted_iota(jnp.int32, rs.shape, 1)
    packed = jnp.where(lane < NORM_HEAD_SLOTS, hi, pltpu.roll(rs - hi, NORM_HEAD_SLOTS, 1))
    tn = t * _dot(packed.astype(BF16), expd) * g
    outs = []
    for j in range(t.shape[1] // V7X_LANES):
        tc = tn[:, j * V7X_LANES:(j + 1) * V7X_LANES]
        tr = tc * c + pltpu.roll(tc, V7X_LANES - half, 1) * s1 + pltpu.roll(tc, half, 1) * s2
        outs.append(tr * scale if scale != 1.0 else tr)
    return jnp.concatenate(outs, axis=1)


def _replicate_heads(t, group):
    lane = lax.broadcasted_iota(jnp.int32, (t.shape[0], V7X_LANES), 1)
    first = lane < HEAD_DIM
    outs = []
    for j in range(t.shape[1] // V7X_LANES):
        tc = t[:, j * V7X_LANES:(j + 1) * V7X_LANES]
        tc_sw = pltpu.roll(tc, HEAD_DIM, 1)
        even = jnp.where(first, tc, tc_sw)
        odd = jnp.where(first, tc_sw, tc)
        outs += [even] * (group // 2) + [odd] * (group // 2)
    return jnp.concatenate(outs, axis=1)


def _qkv_kernel(x_ref, g_ref, w_ref, qg_ref, kg_ref, redq_ref, expq_ref, redk_ref, expk_ref,
                c_ref, s1_ref, s2_ref, q_out, k_out, v_out, *, q_dim, kv_dim, group):
    hn = _rms(x_ref[0], g_ref[...]).astype(BF16)
    c, s1, s2 = c_ref[...], s1_ref[...], s2_ref[...]
    q = _dot(hn, w_ref[0, :, 0:q_dim])
    q_out[0] = _norm_rope(q, redq_ref[...], expq_ref[...], qg_ref[...], c, s1, s2,
                          HEAD_DIM ** -0.5).astype(BF16)
    k = _dot(hn, w_ref[0, :, q_dim:q_dim + kv_dim])
    kr = _norm_rope(k, redk_ref[...], expk_ref[...], kg_ref[...], c, s1, s2, 1.0)
    k_out[0] = _replicate_heads(kr, group).astype(BF16)
    v = _dot(hn, w_ref[0, :, q_dim + kv_dim:q_dim + 2 * kv_dim])
    v_out[0] = _replicate_heads(v, group).astype(BF16)


def _attn_kernel(sink_ref, q_ref, kp_ref, k_ref, kn_ref, vp_ref, v_ref, vn_ref, x_ref, wo_ref,
                 o_ref, o_scr, *, tq, n_kv, group):
    i = pl.program_id(1)
    n = pl.num_programs(1)
    blk = ATTN_BLOCK
    gw = group * HEAD_DIM
    k_all = jnp.concatenate([kp_ref[0], k_ref[0], kn_ref[0]], axis=0)
    v_all = jnp.concatenate([vp_ref[0], v_ref[0], vn_ref[0]], axis=0)
    row = lax.broadcasted_iota(jnp.int32, (group * blk, 3 * blk), 0) % blk
    col = lax.broadcasted_iota(jnp.int32, (group * blk, 3 * blk), 1)
    band = (col >= row) & (col <= row + 2 * blk)
    head_of_lane = lax.broadcasted_iota(jnp.int32, (blk, gw), 1) // HEAD_DIM
    head_of_row = lax.broadcasted_iota(jnp.int32, (group * blk, 1), 0) // blk
    nblk = tq // blk
    for jb in range(nblk):
        valid = band
        if jb == 0:
            valid = valid & ((col >= blk) | (i > 0))
        if jb == nblk - 1:
            valid = valid & ((col < 2 * blk) | (i < n - 1))
        for kh in range(n_kv):
            qg = q_ref[0, jb * blk:(jb + 1) * blk, kh * gw:(kh + 1) * gw]
            kw = k_all[jb * blk:(jb + 3) * blk, kh * gw:(kh + 1) * gw]
            vw = v_all[jb * blk:(jb + 3) * blk, kh * gw:(kh + 1) * gw]
            qs = jnp.concatenate(
                [jnp.where(head_of_lane == hl, qg, jnp.zeros_like(qg)) for hl in range(group)], axis=0)
            sink = jnp.full((group * blk, 1), sink_ref[kh * group + group - 1], F32)
            for hl in range(group - 2, -1, -1):
                sink = jnp.where(head_of_row == hl, sink_ref[kh * group + hl], sink)
            s = jnp.where(valid, _dot_nt(qs, kw), -1e30)
            m = jnp.maximum(jnp.max(s, axis=-1, keepdims=True), sink)
            p = jnp.exp(s - m)
            denom = jnp.sum(p, axis=-1, keepdims=True) + jnp.exp(sink - m)
            o = _dot(p.astype(BF16), vw) / denom
            og = o[(group - 1) * blk:group * blk]
            for hl in range(group - 2, -1, -1):
                og = jnp.where(head_of_lane == hl, o[hl * blk:(hl + 1) * blk], og)
            o_scr[jb * blk:(jb + 1) * blk, kh * gw:(kh + 1) * gw] = og.astype(BF16)
    o_ref[0] = x_ref[0] + _dot(o_scr[...], wo_ref[0])


def _window_attention(x, norm_g, w_qkv, q_g, k_g, sink, w_o, layer, rope,
                      *, tm=QKV_TILE, tq=ATTN_TILE):
    bsz, s_len, d = x.shape
    n_heads = sink.shape[0]
    q_dim = w_o.shape[1]
    kv_dim = (w_qkv.shape[2] - q_dim) // 2
    n_kv = kv_dim // HEAD_DIM
    group = n_heads // n_kv
    tm = min(tm, s_len)
    tq = min(tq, s_len)
    assert s_len % tm == 0 and s_len % tq == 0 and tq % ATTN_BLOCK == 0
    assert n_heads <= NORM_HEAD_SLOTS and group % 2 == 0 and q_dim == n_heads * HEAD_DIM
    c, s1, s2 = rope

    def reduce_expand(width):
        i = lax.broadcasted_iota(jnp.int32, (width, V7X_LANES), 0)
        j = lax.broadcasted_iota(jnp.int32, (width, V7X_LANES), 1)
        red = jnp.where(i // HEAD_DIM == j, 1.0 / HEAD_DIM, 0.0).astype(BF16)
        expd = ((j.T % NORM_HEAD_SLOTS == i.T // HEAD_DIM) & (j.T < 2 * NORM_HEAD_SLOTS)).astype(BF16)
        return red, expd

    redq, expq = reduce_expand(q_dim)
    redk, expk = reduce_expand(kv_dim)
    tile = lambda w: pl.BlockSpec((1, tm, w), lambda b, i: (b, i, 0))
    rope_spec = pl.BlockSpec((tm, V7X_LANES), lambda b, i: (i, 0))
    q, k, v = pl.pallas_call(
        functools.partial(_qkv_kernel, q_dim=q_dim, kv_dim=kv_dim, group=group),
        grid=(bsz, s_len // tm),
        in_specs=[tile(d), _full((1, d)), _layer(w_qkv, layer), _full((1, q_dim)),
                  _full((1, kv_dim)), _full(redq.shape), _full(expq.shape), _full(redk.shape),
                  _full(expk.shape), rope_spec, rope_spec, rope_spec],
        out_specs=[tile(q_dim), tile(q_dim), tile(q_dim)],
        out_shape=[jax.ShapeDtypeStruct((bsz, s_len, q_dim), BF16)] * 3,
        compiler_params=pltpu.CompilerParams(
            dimension_semantics=("parallel", "parallel"), vmem_limit_bytes=V7X_VMEM_LIMIT),
        name="attn_qkv",
    )(x, norm_g.reshape(1, d), w_qkv, jnp.tile(q_g, n_heads).reshape(1, q_dim),
      jnp.tile(k_g, n_kv).reshape(1, kv_dim), redq, expq, redk, expk, c, s1, s2)

    r = tq // ATTN_BLOCK
    last = s_len // ATTN_BLOCK - 1
    prev_spec = pl.BlockSpec((1, ATTN_BLOCK, q_dim), lambda b, i: (b, jnp.maximum(i * r - 1, 0), 0))
    main_spec = pl.BlockSpec((1, tq, q_dim), lambda b, i: (b, i, 0))
    next_spec = pl.BlockSpec((1, ATTN_BLOCK, q_dim), lambda b, i: (b, jnp.minimum((i + 1) * r, last), 0))
    return pl.pallas_call(
        functools.partial(_attn_kernel, tq=tq, n_kv=n_kv, group=group),
        grid=(bsz, s_len // tq),
        in_specs=[pl.BlockSpec(memory_space=pltpu.SMEM), main_spec,
                  prev_spec, main_spec, next_spec, prev_spec, main_spec, next_spec,
                  pl.BlockSpec((1, tq, d), lambda b, i: (b, i, 0)), _layer(w_o, layer)],
        out_specs=pl.BlockSpec((1, tq, d), lambda b, i: (b, i, 0)),
        out_shape=jax.ShapeDtypeStruct(x.shape, F32),
        scratch_shapes=[pltpu.VMEM((tq, q_dim), BF16)],
        compiler_params=pltpu.CompilerParams(
            dimension_semantics=("parallel", "parallel"), vmem_limit_bytes=V7X_VMEM_LIMIT),
        name="attn_core",
    )(sink.astype(F32), q, k, k, k, v, v, v, x, w_o)


def _softplus(x):
    return jnp.maximum(x, 0.0) + jnp.log1p(jnp.exp(-jnp.abs(x)))


def _ssd_in_kernel(xp_ref, x_ref, xn_ref, g_ref, w_ref, wdt_ref, cw_ref, cb_ref, dtb_ref, a_ref, at_ref,
                   z_out, xbc_out, acum_out, tr_out, ew_out, hn_scr, perm_scr, dt_scr,
                   *, tm, fc, d_inner, conv_dim, heads):
    g = g_ref[...]
    _fill_normed(hn_scr, perm_scr, xp_ref, x_ref, xn_ref, g, tm)
    hn_nat = _rms(x_ref[0], g).astype(BF16)
    z_out[0] = _silu(_dot(hn_nat, w_ref[0, :, 0:d_inner])).astype(BF16)
    dt_scr[0] = _softplus(_dot(hn_nat, wdt_ref[...]) + dtb_ref[...])
    _ssd_decay_terms(dt_scr, a_ref, at_ref, acum_out, tr_out, ew_out, heads, tm // SSD_CHUNK)
    hn = hn_scr[...]
    for c in range(conv_dim // fc):
        lo, hi = c * fc, (c + 1) * fc
        h = _dot(hn, w_ref[0, :, d_inner + lo:d_inner + hi])
        y = _unpermute(perm_scr, _dwconv(h, cw_ref[:, lo:hi], cb_ref[:, lo:hi], tm), tm)
        xbc_out[0, :, lo:hi] = _silu(y).astype(BF16)


def _pieces(x, n):
    out = []
    for _ in range(n - 1):
        p = x.astype(BF16).astype(F32)
        out.append(p)
        x = x - p
    out.append(x.astype(BF16).astype(F32))
    return out


def _ssd_decay_terms(dt_ref, a_ref, at_ref, acum_out, tr_out, ew_out, heads, n_chunks):
    lc = SSD_CHUNK
    row = lax.broadcasted_iota(jnp.int32, (lc, lc), 0)
    col = lax.broadcasted_iota(jnp.int32, (lc, lc), 1)
    lower = (row >= col).astype(BF16)
    upper = (row <= col).astype(BF16)
    fwd_lane = lax.broadcasted_iota(jnp.int32, (lc, V7X_LANES), 1) < heads
    fwd_row = lax.broadcasted_iota(jnp.int32, (V7X_LANES, lc), 0) < heads
    for k in range(n_chunks):
        rows = slice(k * lc, (k + 1) * lc)
        dt = dt_ref[0, rows, :]
        dtt = dt.T
        ps = [p.astype(BF16) for p in _pieces(dt * a_ref[...], 3)]
        cum_f = (_dot(lower, ps[0]) + _dot(lower, ps[1])) + _dot(lower, ps[2])
        cum_b = (_dot(upper, ps[0]) + _dot(upper, ps[1])) + _dot(upper, ps[2])
        a_cum = jnp.where(fwd_lane, cum_f, cum_b)
        qs = [q.astype(BF16) for q in _pieces(dtt * at_ref[...], 3)]
        cum_tf = (_dot(qs[0], upper) + _dot(qs[1], upper)) + _dot(qs[2], upper)
        cum_tb = (_dot(qs[0], lower) + _dot(qs[1], lower)) + _dot(qs[2], lower)
        a_cum_t = jnp.where(fwd_row, cum_tf, cum_tb)
        a_end = jnp.where(fwd_lane[0:1], a_cum[lc - 1:lc, :], a_cum[0:1, :])
        e1, e2 = _pieces(jnp.exp(a_cum), 2)
        w1, w2 = _pieces(dt * jnp.exp(a_end - a_cum), 2)
        acum_out[0, rows, :] = a_cum
        half = V7X_LANES // 2
        dsum = jnp.log(dtt[0:heads] + dtt[heads:2 * heads])
        tr_out[0, :, rows] = jnp.concatenate(
            [a_cum_t[0:half] - jnp.log(dtt[0:half]), dsum, jnp.zeros((half - heads, lc), F32)], axis=0)
        for d in range(2):
            lanes = slice(d * heads, (d + 1) * heads)
            ew_out[0, rows, d * V7X_LANES:(d + 1) * V7X_LANES] = jnp.concatenate(
                [e1[:, lanes], e2[:, lanes], w1[:, lanes], w2[:, lanes]], axis=1).astype(BF16)


def _ssd_state_step(xbc_ref, ew, state_ref, rows, g, end, d_inner):
    gw = d_inner // SSD_GROUPS
    gn = SSD_GROUPS * D_STATE
    e_exp = ew[:, g * gw:(g + 1) * gw]
    w_exp = ew[:, d_inner + g * gw:d_inner + (g + 1) * gw]
    b_g = xbc_ref[0, rows, d_inner + g * D_STATE:d_inner + (g + 1) * D_STATE]
    c_g = xbc_ref[0, rows, d_inner + gn + g * D_STATE:d_inner + gn + (g + 1) * D_STATE]
    x_g = xbc_ref[0, rows, g * gw:(g + 1) * gw]
    st = state_ref[g]
    y_off = _dot(c_g, st.astype(BF16)) * e_exp
    xw = (x_g.astype(F32) * w_exp).astype(BF16)
    state_ref[g] = e_exp[end:end + 1, :] * st + _dot_tn(b_g, xw)
    return y_off, b_g, c_g, x_g


def _ssd_intra(acum_ref, tr_ref, rows, g, b_g, c_g, x_g, heads, masks):
    from_fwd, from_bwd, lane_head = masks
    hpg = heads // SSD_GROUPS
    cb = _dot_nt(c_g, b_g)
    ms, xb = [], []
    for r in range(hpg):
        h = g * hpg + r
        zf = acum_ref[0, rows, h:h + 1] - tr_ref[0, h:h + 1, rows]
        zb = acum_ref[0, rows, heads + h:heads + h + 1] - tr_ref[0, heads + h:heads + h + 1, rows]
        zd = tr_ref[0, 2 * heads + h:2 * heads + h + 1, rows]
        z = jnp.where(from_fwd, zf, jnp.where(from_bwd, zb, zd))
        ms.append((cb * jnp.exp(z)).astype(BF16))
        xb.append(jnp.where(lane_head == r, x_g, jnp.zeros_like(x_g)))
    return _dot(jnp.concatenate(ms, axis=1), jnp.concatenate(xb, axis=0))


def _intra_masks(lc, gw):
    row = lax.broadcasted_iota(jnp.int32, (lc, lc), 0)
    col = lax.broadcasted_iota(jnp.int32, (lc, lc), 1)
    lane_head = lax.broadcasted_iota(jnp.int32, (lc, gw), 1) // SSD_HEAD_DIM
    return col < row, col > row, lane_head


def _ssd_bwd_kernel(xbc_ref, acum_ref, tr_ref, ew_ref, xsel_ref, y_out, state_ref,
                    *, d_inner, heads, cps, intra_groups):
    @pl.when(pl.program_id(1) == 0)
    def _():
        state_ref[...] = jnp.zeros_like(state_ref)

    lc = SSD_CHUNK
    gw = d_inner // SSD_GROUPS
    masks = _intra_masks(lc, gw)
    for k in range(cps - 1, -1, -1):
        rows = slice(k * lc, (k + 1) * lc)
        ew = _dot(ew_ref[0, rows, V7X_LANES:2 * V7X_LANES], xsel_ref[...])
        for g in range(SSD_GROUPS):
            y, b_g, c_g, x_g = _ssd_state_step(xbc_ref, ew, state_ref, rows, g, 0, d_inner)
            if g in intra_groups:
                y = y + _ssd_intra(acum_ref, tr_ref, rows, g, b_g, c_g, x_g, heads, masks)
            y_out[0, rows, g * gw:(g + 1) * gw] = y.astype(y_out.dtype)


def _ssd_fwd_kernel(xbc_ref, acum_ref, tr_ref, ew_ref, xsel_ref, yb_ref, z_ref, x_ref,
                    dexp_ref, gg_ref, wout_ref, o_ref, state_ref, yn_scr,
                    *, d_inner, heads, cps, intra_groups):
    @pl.when(pl.program_id(1) == 0)
    def _():
        state_ref[...] = jnp.zeros_like(state_ref)

    gw = d_inner // SSD_GROUPS
    lc = SSD_CHUNK
    masks = _intra_masks(lc, gw)
    for k in range(cps):
        rows = slice(k * lc, (k + 1) * lc)
        ew = _dot(ew_ref[0, rows, 0:V7X_LANES], xsel_ref[...])
        for g in range(SSD_GROUPS):
            cols = slice(g * gw, (g + 1) * gw)
            y, b_g, c_g, x_g = _ssd_state_step(xbc_ref, ew, state_ref, rows, g, lc - 1, d_inner)
            if g in intra_groups:
                y = y + _ssd_intra(acum_ref, tr_ref, rows, g, b_g, c_g, x_g, heads, masks)
            y = y + yb_ref[0, rows, cols].astype(F32) + x_g.astype(F32) * dexp_ref[:, cols]
            y = y * z_ref[0, rows, cols].astype(F32)
            y = y * lax.rsqrt(jnp.mean(y * y, axis=-1, keepdims=True) + EPS)
            yn_scr[rows, cols] = (y * gg_ref[:, cols]).astype(BF16)
    o_ref[0] = x_ref[0] + _dot(yn_scr[...], wout_ref[0])


def _ssd_mixer(x, norm_g, w_in, conv_w, conv_b, dt_bias, a_log, d_skip, gate_g, w_out, layer,
               *, tm=SSD_IN_TILE, fc=SSD_IN_COLS, cps=SSD_CHUNKS_PER_STEP):
    bsz, s_len, d = x.shape
    d_inner = w_out.shape[1]
    heads = d_skip.shape[0]
    conv_dim = conv_w.shape[1]
    tm = min(tm, s_len)
    dt_pad = V7X_LANES - 2 * heads
    w_dt = jnp.pad(w_in[layer, :, d_inner + conv_dim:], ((0, 0), (0, dt_pad)))
    dt_b = jnp.pad(dt_bias.reshape(1, 2 * heads).astype(F32), ((0, 0), (0, dt_pad)))
    a_lanes = jnp.pad((-jnp.exp(a_log.astype(F32))).reshape(1, 2 * heads), ((0, 0), (0, dt_pad)))
    tile = lambda w: pl.BlockSpec((1, tm, w), lambda b, i: (b, i, 0))
    sz, xbc, acum, tr, ew = pl.pallas_call(
        functools.partial(_ssd_in_kernel, tm=tm, fc=fc, d_inner=d_inner, conv_dim=conv_dim, heads=heads),
        grid=(bsz, s_len // tm),
        in_specs=_halo_specs(tm, s_len, d) + [
            _full((1, d)), _layer(w_in, layer), _full((d, V7X_LANES)), _full(conv_w.shape),
            _full((1, conv_dim)), _full((1, V7X_LANES)), _full((1, V7X_LANES)), _full((V7X_LANES, 1))],
        out_specs=[tile(d_inner), tile(conv_dim), tile(V7X_LANES),
                   pl.BlockSpec((1, V7X_LANES, tm), lambda b, i: (b, 0, i)), tile(2 * V7X_LANES)],
        out_shape=[jax.ShapeDtypeStruct((bsz, s_len, d_inner), BF16),
                   jax.ShapeDtypeStruct((bsz, s_len, conv_dim), BF16),
                   jax.ShapeDtypeStruct((bsz, s_len, V7X_LANES), F32),
                   jax.ShapeDtypeStruct((bsz, V7X_LANES, s_len), F32),
                   jax.ShapeDtypeStruct((bsz, s_len, 2 * V7X_LANES), BF16)],
        scratch_shapes=[pltpu.VMEM((tm + 2 * F32_SUBLANES, d), BF16),
                        pltpu.VMEM((d // V7X_LANES, tm, V7X_LANES), F32),
                        pltpu.VMEM((1, tm, V7X_LANES), F32)],
        compiler_params=pltpu.CompilerParams(
            dimension_semantics=("parallel", "parallel"), vmem_limit_bytes=V7X_VMEM_LIMIT),
        name="ssd_in",
    )(x, x, x, norm_g.reshape(1, d), w_in, w_dt, conv_w, conv_b.reshape(1, conv_dim), dt_b,
      a_lanes, a_lanes.reshape(V7X_LANES, 1))

    lc = SSD_CHUNK
    cps = min(cps, s_len // lc)
    assert s_len % tm == 0 and tm % (F32_SUBLANES * F32_SUBLANES) == 0 and conv_dim % fc == 0
    assert tm % lc == 0 and s_len % (cps * lc) == 0 and heads % SSD_GROUPS == 0 and 2 * heads <= V7X_LANES // 2
    assert d_inner == heads * SSD_HEAD_DIM and conv_dim == d_inner + 2 * SSD_GROUPS * D_STATE
    nc = s_len // (lc * cps)
    step = lambda w: pl.BlockSpec((1, cps * lc, w), lambda b, c: (b, c, 0))

    xr = lax.broadcasted_iota(jnp.int32, (4 * heads, 2 * d_inner), 0)
    xc = lax.broadcasted_iota(jnp.int32, (4 * heads, 2 * d_inner), 1)
    xsel = ((xr % heads == (xc % d_inner) // SSD_HEAD_DIM)
            & (xr // (2 * heads) == xc // d_inner)).astype(BF16)
    bwd_share = SSD_BWD_INTRA_GROUPS
    state = pltpu.VMEM((SSD_GROUPS, D_STATE, d_inner // SSD_GROUPS), F32)
    rev = lambda w: pl.BlockSpec((1, cps * lc, w), lambda b, c: (b, nc - 1 - c, 0))
    fwd = step
    y_b = pl.pallas_call(
        functools.partial(_ssd_bwd_kernel, d_inner=d_inner, heads=heads, cps=cps,
                          intra_groups=range(0, bwd_share)),
        grid=(bsz, nc),
        in_specs=[rev(conv_dim), rev(V7X_LANES),
                  pl.BlockSpec((1, V7X_LANES, cps * lc), lambda b, c: (b, 0, nc - 1 - c)),
                  rev(2 * V7X_LANES), _full(xsel.shape)],
        out_specs=rev(d_inner),
        out_shape=jax.ShapeDtypeStruct((bsz, s_len, d_inner), BF16),
        scratch_shapes=[state],
        compiler_params=pltpu.CompilerParams(
            dimension_semantics=("parallel", "arbitrary"), vmem_limit_bytes=V7X_VMEM_LIMIT),
        name="ssd_scan_bwd",
    )(xbc, acum, tr, ew, xsel)

    return pl.pallas_call(
        functools.partial(_ssd_fwd_kernel, d_inner=d_inner, heads=heads, cps=cps,
                          intra_groups=range(bwd_share, SSD_GROUPS)),
        grid=(bsz, nc),
        in_specs=[fwd(conv_dim), fwd(V7X_LANES),
                  pl.BlockSpec((1, V7X_LANES, cps * lc), lambda b, c: (b, 0, c)),
                  fwd(2 * V7X_LANES), _full(xsel.shape), fwd(d_inner), fwd(d_inner), fwd(d),
                  _full((1, d_inner)), _full((1, d_inner)), _layer(w_out, layer)],
        out_specs=fwd(d),
        out_shape=jax.ShapeDtypeStruct(x.shape, F32),
        scratch_shapes=[state, pltpu.VMEM((cps * lc, d_inner), BF16)],
        compiler_params=pltpu.CompilerParams(
            dimension_semantics=("parallel", "arbitrary"), vmem_limit_bytes=V7X_VMEM_LIMIT),
        name="ssd_scan_fwd",
    )(xbc, acum, tr, ew, xsel, y_b, sz, x,
      jnp.broadcast_to(d_skip.astype(F32)[:, None], (heads, SSD_HEAD_DIM)).reshape(1, d_inner),
      gate_g.reshape(1, d_inner).astype(F32), w_out)


def kernel(x, attn_norm, attn_w_qkv, attn_q_norm, attn_k_norm, attn_sink, attn_w_o, ssd_norm, ssd_w_in, ssd_conv_w, ssd_conv_b, ssd_dt_bias, ssd_a_log, ssd_d, ssd_gate_norm, ssd_w_out, ffn_norm, ffn_w_up, ffn_conv_w, ffn_conv_b, ffn_w_down):
    depth = ffn_norm.shape[0]
    rope = _rope_tables(x.shape[1])
    w_qkv, w_o = attn_w_qkv.astype(BF16), attn_w_o.astype(BF16)
    w_in, w_out = ssd_w_in.astype(BF16), ssd_w_out.astype(BF16)
    w_up, w_down = ffn_w_up.astype(BF16), ffn_w_down.astype(BF16)
    for i in range(depth):
        j = i // 2
        if i % 2 == 0:
            x = _window_attention(x, attn_norm[j], w_qkv, attn_q_norm[j], attn_k_norm[j],
                                  attn_sink[j], w_o, j, rope)
        else:
            x = _ssd_mixer(x, ssd_norm[j], w_in, ssd_conv_w[j], ssd_conv_b[j], ssd_dt_bias[j],
                           ssd_a_log[j], ssd_d[j], ssd_gate_norm[j], w_out, j)
        x = _conv_ffn(x, ffn_norm[i], w_up, ffn_conv_w[i], ffn_conv_b[i], w_down, i)
    return x
```
